```python
import jax, jax.numpy as jnp
from jax import lax
import numpy as np

D_MODEL = 1024
BATCH = 8
SEQ = 2048
DEPTH = 2

N_MIXERS = 2
N_MLSTM_LAYERS = (DEPTH + 1) // 2
N_POOL_LAYERS = DEPTH // 2
MLSTM_HEADS = 8
MLSTM_DV = D_MODEL // MLSTM_HEADS
MLSTM_DQK = MLSTM_DV // 2
MLSTM_CHUNK = 64
GATE_SOFTCAP = 15.0
MLSTM_IN_COLS = 2 * MLSTM_HEADS * MLSTM_DQK + 2 * MLSTM_HEADS * MLSTM_DV + 4 * MLSTM_HEADS
POOL_WINDOWS = (2, 4, 8, 16)
POOL_GROUPS = len(POOL_WINDOWS)
POOL_GROUP_DIM = D_MODEL // POOL_GROUPS
D_FF = 4 * D_MODEL
EPS = 1e-6

kernel_name = 'bidir_mlstm_pool_hybrid_trunk'


def _rmsnorm(x, g):
    xf = x.astype(jnp.float32)
    y = xf * lax.rsqrt(jnp.mean(xf * xf, axis=-1, keepdims=True) + EPS)
    return (y * g.astype(jnp.float32)).astype(x.dtype)


def _mlstm_chunkwise(q, k, v, log_i, log_f):
    B, H, S, dk = q.shape
    dv = v.shape[-1]
    L = MLSTM_CHUNK
    nc = S // L

    def to_chunks(t):
        return jnp.moveaxis(t.reshape((B, H, nc, L) + t.shape[3:]), 2, 0)

    xs = tuple(to_chunks(t) for t in (q, k, v, log_i, log_f))
    lower = jnp.tril(jnp.ones((L, L), dtype=bool))

    def step(carry, inp):
        C, n, m = carry
        qj, kj, vj, ij, fj = inp
        b = jnp.cumsum(fj, axis=-1)
        d = b[..., :, None] - b[..., None, :] + ij[..., None, :]
        d = jnp.where(lower, d, -jnp.inf)
        m_inter = b + m[..., None]
        m_t = jnp.maximum(m_inter, jnp.max(d, axis=-1))
        s = jnp.einsum('bhtd,bhsd->bhts', qj, kj) * jnp.exp(d - m_t[..., None])
        sc = jnp.exp(m_inter - m_t)
        num = jnp.einsum('bhts,bhsv->bhtv', s, vj) + sc[..., None] * jnp.einsum('bhtd,bhdv->bhtv', qj, C)
        den = jnp.sum(s, axis=-1) + sc * jnp.einsum('bhtd,bhd->bht', qj, n)
        h = num / jnp.maximum(jnp.abs(den), jnp.exp(-m_t))[..., None]
        b_last = b[..., -1]
        g = b_last[..., None] - b + ij
        m_new = jnp.maximum(b_last + m, jnp.max(g, axis=-1))
        decay = jnp.exp(b_last + m - m_new)
        wk = jnp.exp(g - m_new[..., None])
        C_new = decay[..., None, None] * C + jnp.einsum('bhs,bhsd,bhsv->bhdv', wk, kj, vj)
        n_new = decay[..., None] * n + jnp.einsum('bhs,bhsd->bhd', wk, kj)
        return (C_new, n_new, m_new), h

    init = (jnp.zeros((B, H, dk, dv), jnp.float32),
            jnp.zeros((B, H, dk), jnp.float32),
            jnp.zeros((B, H), jnp.float32))
    _, hc = lax.scan(step, init, xs)
    return jnp.moveaxis(hc, 0, 2).reshape(B, H, S, dv)


def _mlstm_mixer(u, w_in, gate_b, head_g, w_out):
    B, S, _ = u.shape
    H, dk, dv = MLSTM_HEADS, MLSTM_DQK, MLSTM_DV
    proj = u @ w_in
    cuts = [H * dk, 2 * H * dk, 2 * H * dk + H * dv, 2 * H * dk + 2 * H * dv]
    q, k, v, o, gates = jnp.split(proj, cuts, axis=-1)

    def heads(t, d):
        return t.reshape(B, S, H, d).transpose(0, 2, 1, 3).astype(jnp.float32)

    q = heads(q, dk) * (dk ** -0.5)
    k = heads(k, dk)
    v = heads(v, dv)
    g = gates.astype(jnp.float32) + gate_b.astype(jnp.float32)
    g = GATE_SOFTCAP * jnp.tanh(g / GATE_SOFTCAP)
    g = g.reshape(B, S, 4, H).transpose(2, 0, 3, 1)
    h_fwd = _mlstm_chunkwise(q, k, v, g[0], jax.nn.log_sigmoid(g[1]))

    def flip(t):
        return jnp.flip(t, axis=2)

    h_bwd = flip(_mlstm_chunkwise(flip(q), flip(k), flip(v), flip(g[2]), flip(jax.nn.log_sigmoid(g[3]))))
    h = h_fwd + h_bwd
    h = h * lax.rsqrt(jnp.mean(h * h, axis=-1, keepdims=True) + EPS)
    h = h.transpose(0, 2, 1, 3).reshape(B, S, H * dv) * head_g.astype(jnp.float32)
    h = (h * jax.nn.sigmoid(o.astype(jnp.float32))).astype(u.dtype)
    return h @ w_out


def _centred_mean(a, w):
    B, S, C = a.shape
    csum = jnp.concatenate([jnp.zeros((B, 1, C), jnp.float32),
                            jnp.cumsum(a.astype(jnp.float32), axis=1)], axis=1)
    t = np.arange(S)
    lo = np.clip(t - w // 2, 0, S)
    hi = np.clip(t + w - w // 2, 0, S)
    count = (hi - lo).astype(np.float32)
    return (csum[:, hi] - csum[:, lo]) / count[None, :, None]


def _pool_mixer(u, w_in, w_group, w_out, scale):
    B, S, _ = u.shape
    a = u @ w_in
    groups = jnp.split(a, POOL_GROUPS, axis=-1)
    pooled = jnp.stack([_centred_mean(gp, w) - gp.astype(jnp.float32)
                        for gp, w in zip(groups, POOL_WINDOWS)], axis=2)
    mixed = jnp.einsum('bsgc,gcd->bsgd', pooled.astype(u.dtype), w_group).reshape(B, S, D_MODEL)
    return (mixed @ w_out) * scale


def _mlp(u, w1, w2):
    return jnp.square(jax.nn.relu(u @ w1)) @ w2


def setup_inputs(seed: int = 0) -> dict:
    key = jax.random.key(seed)
    ks = jax.random.split(key, 16)
    D = D_MODEL

    def nrm(k, shape, fan_in):
        return jax.random.normal(k, shape, jnp.float32) * (fan_in ** -0.5)

    def gain(k, shape):
        return 1.0 + 0.02 * jax.random.normal(k, shape, jnp.float32)

    x = jax.random.normal(ks[0], (BATCH, SEQ, D), jnp.float32)
    mix_norm_g = gain(ks[1], (DEPTH, D))
    mlp_norm_g = gain(ks[2], (DEPTH, D))
    mlstm_w_in = nrm(ks[3], (N_MLSTM_LAYERS, D, MLSTM_IN_COLS), D)
    f_base = jnp.linspace(3.0, 6.0, MLSTM_HEADS, dtype=jnp.float32)
    zero = jnp.zeros_like(f_base)
    base = jnp.stack([zero, f_base, zero, f_base])[None]
    mlstm_gate_b = (base + 0.1 * jax.random.normal(ks[4], (N_MLSTM_LAYERS, 4, MLSTM_HEADS), jnp.float32)
                    ).reshape(N_MLSTM_LAYERS, 4 * MLSTM_HEADS)
    mlstm_head_g = gain(ks[5], (N_MLSTM_LAYERS, MLSTM_HEADS * MLSTM_DV))
    mlstm_w_out = nrm(ks[6], (N_MLSTM_LAYERS, MLSTM_HEADS * MLSTM_DV, D), MLSTM_HEADS * MLSTM_DV)
    pool_w_in = nrm(ks[7], (N_POOL_LAYERS, D, D), D)
    pool_w_group = nrm(ks[8], (N_POOL_LAYERS, POOL_GROUPS, POOL_GROUP_DIM, POOL_GROUP_DIM), POOL_GROUP_DIM)
    pool_w_out = nrm(ks[9], (N_POOL_LAYERS, D, D), D)
    pool_scale = gain(ks[10], (N_POOL_LAYERS, D))
    mlp_w1 = nrm(ks[11], (DEPTH, D, D_FF), D)
    mlp_w2 = nrm(ks[12], (DEPTH, D_FF, D), D_FF)
    final_norm_g = gain(ks[13], (D,))
    return {'x': x, 'mix_norm_g': mix_norm_g, 'mlp_norm_g': mlp_norm_g,
            'mlstm_w_in': mlstm_w_in, 'mlstm_gate_b': mlstm_gate_b, 'mlstm_head_g': mlstm_head_g,
            'mlstm_w_out': mlstm_w_out, 'pool_w_in': pool_w_in, 'pool_w_group': pool_w_group,
            'pool_w_out': pool_w_out, 'pool_scale': pool_scale, 'mlp_w1': mlp_w1, 'mlp_w2': mlp_w2,
            'final_norm_g': final_norm_g}


def reference(x, mix_norm_g, mlp_norm_g, mlstm_w_in, mlstm_gate_b, mlstm_head_g, mlstm_w_out,
              pool_w_in, pool_w_group, pool_w_out, pool_scale, mlp_w1, mlp_w2, final_norm_g):
    for i in range(DEPTH):
        j = i // N_MIXERS
        u = _rmsnorm(x, mix_norm_g[i])
        if i % N_MIXERS == 0:
            x = x + _mlstm_mixer(u, mlstm_w_in[j], mlstm_gate_b[j], mlstm_head_g[j], mlstm_w_out[j])
        else:
            x = x + _pool_mixer(u, pool_w_in[j], pool_w_group[j], pool_w_out[j], pool_scale[j])
        u = _rmsnorm(x, mlp_norm_g[i])
        x = x + _mlp(u, mlp_w1[i], mlp_w2[i])
    return _rmsnorm(x, final_norm_g)
```

```python
import functools

import numpy as np
import jax
import jax.numpy as jnp
from jax import lax
from jax.experimental import pallas as pl
from jax.experimental.pallas import tpu as pltpu

F32 = jnp.float32
BF16 = jnp.bfloat16

HEADS = 8
DQK = 64
DV = 128
CHUNK = 256
GATE_SOFTCAP = 15.0
POOL_WINDOWS = (2, 4, 8, 16)
POOL_HALO = 8
EPS = 1e-6
GATE_LANES = 128
VMEM_LIMIT = 56 * 1024 * 1024

G_CF, G_CB, G_EF, G_SB, G_WKF, G_WKB, G_DECF, G_DECB, G_AF, G_AB = range(10)


def _rms(x, g):
    ms = jnp.mean(x * x, axis=-1, keepdims=True)
    return x * lax.rsqrt(ms + EPS) * g


def _dot(a, b):
    return jnp.dot(a, b, preferred_element_type=F32)


def _dot_tn(a, b):
    return lax.dot_general(a, b, (((0,), (0,)), ((), ())), preferred_element_type=F32)


def _dot_nt(a, b):
    return lax.dot_general(a, b, (((1,), (1,)), ((), ())), preferred_element_type=F32)


def _split3(x):
    hi = x.astype(BF16)
    r1 = x - hi.astype(F32)
    mid = r1.astype(BF16)
    lo = (r1 - mid.astype(F32)).astype(BF16)
    return hi, mid, lo


def _dot3(parts, m):
    return _dot(parts[0], m) + _dot(parts[1], m) + _dot(parts[2], m)


def _dot3_left(m, parts):
    return _dot(m, parts[0]) + _dot(m, parts[1]) + _dot(m, parts[2])


def _const_spec(shape):
    zeros = (0,) * len(shape)
    return pl.BlockSpec(shape, lambda *_: zeros, pipeline_mode=pl.Buffered(1))


def _params(n_axes):
    return pltpu.CompilerParams(dimension_semantics=("arbitrary",) * n_axes,
                                vmem_limit_bytes=VMEM_LIMIT)


def _proj_kernel(x_ref, g_ref, wqk_ref, wv_ref, wo_ref, wg_ref, gb_ref,
                 qk_ref, v_ref, og_ref, gate_ref):
    u = _rms(x_ref[...], g_ref[...]).astype(BF16)
    qk_ref[...] = _dot(u, wqk_ref[...]).astype(BF16)
    v_ref[...] = _dot(u, wv_ref[...]).astype(BF16)
    og_ref[...] = jax.nn.sigmoid(_dot(u, wo_ref[...])).astype(BF16)
    g = _dot(u, wg_ref[...]) + gb_ref[...]
    gate_ref[...] = GATE_SOFTCAP * jnp.tanh(g / GATE_SOFTCAP)


def _proj_call(x2, g, wqk, wv, wo, wg, gb, tm):
    m, d = x2.shape
    hd = HEADS * DV
    row = lambda n: pl.BlockSpec((tm, n), lambda i: (i, 0))
    return pl.pallas_call(
        _proj_kernel,
        grid=(m // tm,),
        in_specs=[row(d), _const_spec((1, d)), _const_spec(wqk.shape), _const_spec(wv.shape),
                  _const_spec(wo.shape), _const_spec(wg.shape), _const_spec((1, GATE_LANES))],
        out_specs=[row(2 * HEADS * DQK), row(hd), row(hd), row(GATE_LANES)],
        out_shape=[jax.ShapeDtypeStruct((m, 2 * HEADS * DQK), BF16),
                   jax.ShapeDtypeStruct((m, hd), BF16),
                   jax.ShapeDtypeStruct((m, hd), BF16),
                   jax.ShapeDtypeStruct((m, GATE_LANES), F32)],
        compiler_params=_params(1),
        name="mlstm_proj",
    )(x2, g, wqk, wv, wo, wg, gb)


def _gate_routing():
    n = GATE_LANES
    p_tri = np.zeros((n, n), np.float32)
    p_tot = np.zeros((n, n), np.float32)
    p_id = np.zeros((n, n), np.float32)
    i_f, lf_f, i_b, lf_b = 0, 1, 2, 3

    def put(p, src, dst, sign):
        for h in range(HEADS):
            p[src * HEADS + h, dst * HEADS + h] += sign

    for dst in (G_CF, G_EF):
        put(p_tri, lf_f, dst, 1.0)
    for dst in (G_WKF, G_AF):
        put(p_tri, lf_f, dst, -1.0)
        put(p_id, i_f, dst, 1.0)
    for dst in (G_CB, G_SB):
        put(p_tri, lf_b, dst, -1.0)
        put(p_id, lf_b, dst, 1.0)
    for dst in (G_WKB, G_AB):
        put(p_tri, lf_b, dst, 1.0)
        put(p_id, i_b, dst, 1.0)
        put(p_id, lf_b, dst, -1.0)
    put(p_tot, lf_b, G_SB, 1.0)
    put(p_tot, lf_f, G_WKF, 1.0)
    put(p_tot, lf_f, G_DECF, 1.0)
    put(p_tot, lf_b, G_DECB, 1.0)
    return p_tri, p_tot, p_id


def _gate_kernel(gate_ref, tri_ref, ptri_ref, ptot_ref, pid_ref, tab_ref, row_ref):
    seq = gate_ref.shape[0]
    lane = lax.broadcasted_iota(jnp.int32, (CHUNK, GATE_LANES), 1)
    grp = lane // HEADS
    is_forget = (grp == 1) | (grp == 3)
    is_exp = (grp >= G_EF) & (grp <= G_DECB)
    for c in range(seq // CHUNK):
        rows = slice(c * CHUNK, (c + 1) * CHUNK)
        z = gate_ref[rows, :]
        log_sig = jnp.minimum(z, 0.0) - jnp.log1p(jnp.exp(-jnp.abs(z)))
        z = jnp.where(is_forget, log_sig, z)
        zs = _split3(z)
        y_tri = _dot3(zs, ptri_ref[...])
        y_tot = _dot3(zs, ptot_ref[...])
        y_id = _dot3(zs, pid_ref[...])
        pre = (_dot3_left(tri_ref[...], _split3(y_tri))
               + jnp.sum(y_tot, axis=0, keepdims=True) + y_id)
        tab_ref[rows, :] = jnp.where(is_exp, jnp.exp(pre), pre)
        row_ref[0, c] = pre.T[G_AF * HEADS:(G_AB + 1) * HEADS, :]


def _gate_call(gates, batch, seq):
    tri = jnp.asarray(np.tril(np.ones((CHUNK, CHUNK), np.float32)), BF16)
    p_tri, p_tot, p_id = (jnp.asarray(p, BF16) for p in _gate_routing())
    return pl.pallas_call(
        _gate_kernel,
        grid=(batch,),
        in_specs=[pl.BlockSpec((seq, GATE_LANES), lambda b: (b, 0)),
                  _const_spec((CHUNK, CHUNK)), _const_spec((GATE_LANES, GATE_LANES)),
                  _const_spec((GATE_LANES, GATE_LANES)), _const_spec((GATE_LANES, GATE_LANES))],
        out_specs=[pl.BlockSpec((seq, GATE_LANES), lambda b: (b, 0)),
                   pl.BlockSpec((1, seq // CHUNK, 2 * HEADS, CHUNK), lambda b: (b, 0, 0, 0))],
        out_shape=[jax.ShapeDtypeStruct((batch * seq, GATE_LANES), F32),
                   jax.ShapeDtypeStruct((batch, seq // CHUNK, 2 * HEADS, CHUNK), F32)],
        compiler_params=_params(1),
        name="mlstm_gates",
    )(gates, tri, p_tri, p_tot, p_id)


def _mlstm_kernel(qk_ref, v_ref, og_ref, tab_ref, row_ref, hg_ref, out_ref, sf_ref, sb_ref):
    seq = qk_ref.shape[0]
    nc = seq // CHUNK
    ones_blk = jnp.ones((CHUNK, DV), BF16)
    t_idx = lax.broadcasted_iota(jnp.int32, (CHUNK, CHUNK), 0)
    s_idx = lax.broadcasted_iota(jnp.int32, (CHUNK, CHUNK), 1)
    lower = s_idx <= t_idx
    upper = s_idx >= t_idx

    def col(tab, group, h):
        j = group * HEADS + h
        return tab[:, j:j + 1]

    for h in range(HEADS):
        q_lanes = slice(h * 2 * DQK, h * 2 * DQK + DQK)
        k_lanes = slice(h * 2 * DQK + DQK, (h + 1) * 2 * DQK)
        v_lanes = slice(h * DV, (h + 1) * DV)

        def state_step(c, state, store_ref, g_wk, g_dec):
            r0 = pl.multiple_of(c * CHUNK, CHUNK)
            rows = pl.ds(r0, CHUNK)
            store_ref[c] = state.astype(BF16)
            tab = tab_ref[rows, :]
            kw = (qk_ref[rows, k_lanes].astype(F32) * col(tab, g_wk, h)).astype(BF16)
            vaug = jnp.concatenate([v_ref[rows, v_lanes], ones_blk], axis=1)
            dec = col(tab, g_dec, h)[0:1, :]
            return dec * state + _dot_tn(kw, vaug)

        zero_state = jnp.zeros((DQK, 2 * DV), F32)
        lax.fori_loop(0, nc, functools.partial(state_step, store_ref=sf_ref, g_wk=G_WKF, g_dec=G_DECF),
                      zero_state)
        lax.fori_loop(0, nc, lambda i, st: state_step(nc - 1 - i, st, sb_ref, G_WKB, G_DECB),
                      zero_state)

        def out_step(c, carry):
            r0 = pl.multiple_of(c * CHUNK, CHUNK)
            rows = pl.ds(r0, CHUNK)
            q = qk_ref[rows, q_lanes]
            k = qk_ref[rows, k_lanes]
            s = _dot_nt(q, k)
            tab = tab_ref[rows, :]
            a_f = row_ref[0, c, h:h + 1, :]
            a_b = row_ref[0, c, HEADS + h:HEADS + h + 1, :]
            d_f = jnp.exp(jnp.where(lower, col(tab, G_CF, h) + a_f, -jnp.inf))
            d_b = jnp.exp(jnp.where(upper, col(tab, G_CB, h) + a_b, -jnp.inf))
            vaug = jnp.concatenate([v_ref[rows, v_lanes], ones_blk], axis=1)
            nd_f = _dot((s * d_f).astype(BF16), vaug) + col(tab, G_EF, h) * _dot(q, sf_ref[c])
            nd_b = _dot((s * d_b).astype(BF16), vaug) + col(tab, G_SB, h) * _dot(q, sb_ref[c])
            hh = (nd_f[:, :DV] / jnp.maximum(jnp.abs(nd_f[:, DV:]), 1.0)
                  + nd_b[:, :DV] / jnp.maximum(jnp.abs(nd_b[:, DV:]), 1.0))
            hn = hh * lax.rsqrt(jnp.mean(hh * hh, axis=-1, keepdims=True) + EPS)
            gated = hn * hg_ref[:, v_lanes] * og_ref[rows, v_lanes].astype(F32)
            out_ref[rows, v_lanes] = gated.astype(BF16)
            return carry

        lax.fori_loop(0, nc, out_step, 0)


def _mlstm_call(qk, v, og, tab, rowtab, head_g, batch, seq):
    hd = HEADS * DV
    nc = seq // CHUNK
    seq_blk = lambda n: pl.BlockSpec((seq, n), lambda b: (b, 0))
    return pl.pallas_call(
        _mlstm_kernel,
        grid=(batch,),
        in_specs=[seq_blk(2 * HEADS * DQK), seq_blk(hd), seq_blk(hd), seq_blk(GATE_LANES),
                  pl.BlockSpec((1, nc, 2 * HEADS, CHUNK), lambda b: (b, 0, 0, 0)), _const_spec((1, hd))],
        out_specs=seq_blk(hd),
        out_shape=jax.ShapeDtypeStruct((batch * seq, hd), BF16),
        scratch_shapes=[pltpu.VMEM((nc, DQK, 2 * DV), BF16), pltpu.VMEM((nc, DQK, 2 * DV), BF16)],
        compiler_params=_params(1),
        name="mlstm_core",
    )(qk, v, og, tab, rowtab, head_g)


def _mlp_kernel(*refs, has_scale, final_norm, ff_chunk):
    refs = list(refs)
    x_ref, a_ref, wo_ref = refs[:3]
    del refs[:3]
    sc_ref = refs.pop(0) if has_scale else None
    g_ref, w1_ref, w2_ref = refs[:3]
    del refs[:3]
    fg_ref = refs.pop(0) if final_norm else None
    (out_ref,) = refs

    r = _dot(a_ref[...], wo_ref[...])
    if has_scale:
        r = r * sc_ref[...]
    x1 = x_ref[...] + r
    u = _rms(x1, g_ref[...]).astype(BF16)
    y = x1
    for c in range(w1_ref.shape[1] // ff_chunk):
        cols = slice(c * ff_chunk, (c + 1) * ff_chunk)
        hid = jnp.square(jnp.maximum(_dot(u, w1_ref[:, cols]), 0.0)).astype(BF16)
        y = y + _dot(hid, w2_ref[cols, :])
    if final_norm:
        y = _rms(y, fg_ref[...])
    out_ref[...] = y


def _mlp_call(x2, a, wo, scale, g, w1, w2, final_g, tm, ff_chunk=1024):
    m, d = x2.shape
    row = lambda n: pl.BlockSpec((tm, n), lambda i: (i, 0))
    args = [x2, a, wo]
    specs = [row(d), row(a.shape[1]), _const_spec(wo.shape)]
    if scale is not None:
        args.append(scale)
        specs.append(_const_spec((1, d)))
    args += [g, w1, w2]
    specs += [_const_spec((1, d)), _const_spec(w1.shape), _const_spec(w2.shape)]
    if final_g is not None:
        args.append(final_g)
        specs.append(_const_spec((1, d)))
    body = functools.partial(_mlp_kernel, has_scale=scale is not None,
                             final_norm=final_g is not None, ff_chunk=ff_chunk)
    return pl.pallas_call(
        body,
        grid=(m // tm,),
        in_specs=specs,
        out_specs=row(d),
        out_shape=jax.ShapeDtypeStruct((m, d), F32),
        compiler_params=_params(1),
        name="mlp",
    )(*args)


def _pool_kernel(x_ref, g_ref, win_ref, wgrp_ref, out_ref, a_scr, *, proj_rows, pool_rows):
    seq, d = x_ref.shape
    gdim = d // len(POOL_WINDOWS)
    a_scr[0:POOL_HALO, :] = jnp.zeros((POOL_HALO, d), F32)
    a_scr[POOL_HALO + seq:, :] = jnp.zeros((POOL_HALO, d), F32)
    for t in range(seq // proj_rows):
        rows = slice(t * proj_rows, (t + 1) * proj_rows)
        u = _rms(x_ref[rows, :], g_ref[...]).astype(BF16)
        a_scr[POOL_HALO + t * proj_rows:POOL_HALO + (t + 1) * proj_rows, :] = _dot(u, win_ref[...])

    n = pool_rows + 2 * POOL_HALO
    for t in range(seq // pool_rows):
        t0 = t * pool_rows
        pos = t0 + lax.broadcasted_iota(jnp.int32, (pool_rows, gdim), 0)
        for gi, w in enumerate(POOL_WINDOWS):
            cols = slice(gi * gdim, (gi + 1) * gdim)
            xw = a_scr[t0:t0 + n, cols]
            s = xw + pltpu.roll(xw, 1, axis=0)
            step = 1
            while 2 * step < w:
                s = pltpu.roll(s, step, axis=0) + pltpu.roll(s, n - step, axis=0)
                step *= 2
            count = (jnp.minimum(pos + w // 2, seq) - jnp.maximum(pos - w // 2, 0)).astype(F32)
            centre = slice(POOL_HALO, POOL_HALO + pool_rows)
            pooled = s[centre, :] / count - xw[centre, :]
            mixed = _dot(pooled.astype(BF16), wgrp_ref[gi])
            out_ref[t0:t0 + pool_rows, cols] = mixed.astype(BF16)


def _pool_call(x2, g, w_in, w_group, batch, seq, proj_rows=512, pool_rows=256):
    d = x2.shape[1]
    seq_blk = pl.BlockSpec((seq, d), lambda b: (b, 0))
    body = functools.partial(_pool_kernel, proj_rows=proj_rows, pool_rows=pool_rows)
    return pl.pallas_call(
        body,
        grid=(batch,),
        in_specs=[seq_blk, _const_spec((1, d)), _const_spec(w_in.shape), _const_spec(w_group.shape)],
        out_specs=seq_blk,
        out_shape=jax.ShapeDtypeStruct((batch * seq, d), BF16),
        scratch_shapes=[pltpu.VMEM((seq + 2 * POOL_HALO, d), F32)],
        compiler_params=_params(1),
        name="pool_mixer",
    )(x2, g, w_in, w_group)


def kernel(x, mix_norm_g, mlp_norm_g, mlstm_w_in, mlstm_gate_b, mlstm_head_g, mlstm_w_out,
           pool_w_in, pool_w_group, pool_w_out, pool_scale, mlp_w1, mlp_w2, final_norm_g):
    batch, seq, d = x.shape
    assert seq % CHUNK == 0 and d == HEADS * DV
    m = batch * seq
    tm = 512
    x2 = x.reshape(m, d)
    vec = lambda a: a.reshape(1, -1).astype(F32)

    w_in = mlstm_w_in[0]
    nqk = HEADS * DQK
    wq = (w_in[:, :nqk] * (DQK ** -0.5)).reshape(d, HEADS, DQK)
    wk = w_in[:, nqk:2 * nqk].reshape(d, HEADS, DQK)
    wqk = jnp.concatenate([wq, wk], axis=2).reshape(d, 2 * nqk).astype(BF16)
    wv = w_in[:, 2 * nqk:2 * nqk + d].astype(BF16)
    wo = w_in[:, 2 * nqk + d:2 * nqk + 2 * d].astype(BF16)
    n_gate = 4 * HEADS
    wg = jnp.pad(w_in[:, 2 * nqk + 2 * d:], ((0, 0), (0, GATE_LANES - n_gate))).astype(BF16)
    gb = jnp.pad(mlstm_gate_b[0].astype(F32), (0, GATE_LANES - n_gate)).reshape(1, GATE_LANES)

    qk, v, og, gates = _proj_call(x2, vec(mix_norm_g[0]), wqk, wv, wo, wg, gb, tm)
    tab, rowtab = _gate_call(gates, batch, seq)
    hmix = _mlstm_call(qk, v, og, tab, rowtab, vec(mlstm_head_g[0]), batch, seq)
    x2 = _mlp_call(x2, hmix, mlstm_w_out[0].astype(BF16), None, vec(mlp_norm_g[0]),
                   mlp_w1[0].astype(BF16), mlp_w2[0].astype(BF16), None, tm)

    mixed = _pool_call(x2, vec(mix_norm_g[1]), pool_w_in[0].astype(BF16),
                       pool_w_group[0].astype(BF16), batch, seq)
    x2 = _mlp_call(x2, mixed, pool_w_out[0].astype(BF16), vec(pool_scale[0]), vec(mlp_norm_g[1]),
                   mlp_w1[1].astype(BF16), mlp_w2[1].astype(BF16), vec(final_norm_g), tm)
    return x2.reshape(batch, seq, d)
```

```python
import functools

import numpy as np
import jax
import jax.numpy as jnp
from jax import lax
from jax.experimental import pallas as pl
from jax.experimental.pallas import tpu as pltpu

F32 = jnp.float32
BF16 = jnp.bfloat16

HEADS = 8
DQK = 64
DV = 128
CHUNK = 256
GATE_SOFTCAP = 15.0
POOL_WINDOWS = (2, 4, 8, 16)
POOL_HALO = 8
EPS = 1e-6
GATE_LANES = 128
VMEM_LIMIT = 56 * 1024 * 1024

G_CF, G_CB, G_EF, G_SB, G_WKF, G_WKB, G_DECF, G_DECB, G_AF, G_AB = range(10)
R_AF, R_AB, R_WKF, R_WKB = range(4)
ROW_TABLE_ROWS = 4 * HEADS


def _rms(x, g):
    ms = jnp.mean(x * x, axis=-1, keepdims=True)
    return x * lax.rsqrt(ms + EPS) * g


def _dot(a, b):
    return jnp.dot(a, b, preferred_element_type=F32)


def _dot_tn(a, b):
    return lax.dot_general(a, b, (((0,), (0,)), ((), ())), preferred_element_type=F32)


def _dot_nt(a, b):
    return lax.dot_general(a, b, (((1,), (1,)), ((), ())), preferred_element_type=F32)


def _split3(x):
    hi = x.astype(BF16)
    r1 = x - hi.astype(F32)
    mid = r1.astype(BF16)
    lo = (r1 - mid.astype(F32)).astype(BF16)
    return hi, mid, lo


def _dot3(parts, m):
    return _dot(parts[0], m) + _dot(parts[1], m) + _dot(parts[2], m)


def _dot3_left(m, parts):
    return _dot(m, parts[0]) + _dot(m, parts[1]) + _dot(m, parts[2])


def _const_spec(shape):
    zeros = (0,) * len(shape)
    return pl.BlockSpec(shape, lambda *_: zeros, pipeline_mode=pl.Buffered(1))


def _params(n_axes):
    return pltpu.CompilerParams(dimension_semantics=("arbitrary",) * n_axes,
                                vmem_limit_bytes=VMEM_LIMIT)


def _proj_kernel(x_ref, g_ref, wq_ref, wkt_ref, wv_ref, wo_ref, wg_ref, gb_ref,
                 q_ref, kt_ref, v_ref, og_ref, gate_ref):
    u = _rms(x_ref[...], g_ref[...]).astype(BF16)
    q_ref[...] = _dot(u, wq_ref[...]).astype(BF16)
    kt = _dot_nt(wkt_ref[...], u).astype(BF16)
    for c in range(kt_ref.shape[1]):
        kt_ref[0, c] = kt[:, c * CHUNK:(c + 1) * CHUNK]
    v_ref[...] = _dot(u, wv_ref[...]).astype(BF16)
    og_ref[...] = jax.nn.sigmoid(_dot(u, wo_ref[...])).astype(BF16)
    g = _dot(u, wg_ref[...]) + gb_ref[...]
    gate_ref[...] = GATE_SOFTCAP * jnp.tanh(g / GATE_SOFTCAP)


def _proj_call(x2, g, wq, wkt, wv, wo, wg, gb, tm, batch, seq):
    m, d = x2.shape
    hd = HEADS * DV
    nqk = HEADS * DQK
    assert tm % CHUNK == 0 and seq % tm == 0
    tiles_per_seq = seq // tm
    row = lambda n: pl.BlockSpec((tm, n), lambda i: (i, 0))
    kt_spec = pl.BlockSpec((1, tm // CHUNK, nqk, CHUNK),
                           lambda i: (i // tiles_per_seq, i % tiles_per_seq, 0, 0))
    return pl.pallas_call(
        _proj_kernel,
        grid=(m // tm,),
        in_specs=[row(d), _const_spec((1, d)), _const_spec(wq.shape), _const_spec(wkt.shape),
                  _const_spec(wv.shape), _const_spec(wo.shape), _const_spec(wg.shape),
                  _const_spec((1, GATE_LANES))],
        out_specs=[row(nqk), kt_spec, row(hd), row(hd), row(GATE_LANES)],
        out_shape=[jax.ShapeDtypeStruct((m, nqk), BF16),
                   jax.ShapeDtypeStruct((batch, seq // CHUNK, nqk, CHUNK), BF16),
                   jax.ShapeDtypeStruct((m, hd), BF16),
                   jax.ShapeDtypeStruct((m, hd), BF16),
                   jax.ShapeDtypeStruct((m, GATE_LANES), F32)],
        compiler_params=_params(1),
        name="mlstm_proj",
    )(x2, g, wq, wkt, wv, wo, wg, gb)


def _gate_routing():
    n = GATE_LANES
    p_tri = np.zeros((n, n), np.float32)
    p_tot = np.zeros((n, n), np.float32)
    p_id = np.zeros((n, n), np.float32)
    i_f, lf_f, i_b, lf_b = 0, 1, 2, 3

    def put(p, src, dst, sign):
        for h in range(HEADS):
            p[src * HEADS + h, dst * HEADS + h] += sign

    for dst in (G_CF, G_EF):
        put(p_tri, lf_f, dst, 1.0)
    for dst in (G_WKF, G_AF):
        put(p_tri, lf_f, dst, -1.0)
        put(p_id, i_f, dst, 1.0)
    for dst in (G_CB, G_SB):
        put(p_tri, lf_b, dst, -1.0)
        put(p_id, lf_b, dst, 1.0)
    for dst in (G_WKB, G_AB):
        put(p_tri, lf_b, dst, 1.0)
        put(p_id, i_b, dst, 1.0)
        put(p_id, lf_b, dst, -1.0)
    put(p_tot, lf_b, G_SB, 1.0)
    put(p_tot, lf_f, G_WKF, 1.0)
    put(p_tot, lf_f, G_DECF, 1.0)
    put(p_tot, lf_b, G_DECB, 1.0)
    return p_tri, p_tot, p_id


def _gate_kernel(gate_ref, tri_ref, ptri_ref, ptot_ref, pid_ref, tab_ref, row_ref):
    seq = gate_ref.shape[0]
    lane = lax.broadcasted_iota(jnp.int32, (CHUNK, GATE_LANES), 1)
    grp = lane // HEADS
    is_forget = (grp == 1) | (grp == 3)
    is_exp = (grp >= G_EF) & (grp <= G_DECB)
    for c in range(seq // CHUNK):
        rows = slice(c * CHUNK, (c + 1) * CHUNK)
        z = gate_ref[rows, :]
        log_sig = jnp.minimum(z, 0.0) - jnp.log1p(jnp.exp(-jnp.abs(z)))
        z = jnp.where(is_forget, log_sig, z)
        zs = _split3(z)
        y_tri = _dot3(zs, ptri_ref[...])
        y_tot = _dot3(zs, ptot_ref[...])
        y_id = _dot3(zs, pid_ref[...])
        pre = (_dot3_left(tri_ref[...], _split3(y_tri))
               + jnp.sum(y_tot, axis=0, keepdims=True) + y_id)
        tab = jnp.where(is_exp, jnp.exp(pre), pre)
        tab_ref[rows, :] = tab
        tab_t = tab.T
        row_ref[0, c, 0:2 * HEADS, :] = tab_t[G_AF * HEADS:(G_AB + 1) * HEADS, :]
        row_ref[0, c, 2 * HEADS:4 * HEADS, :] = tab_t[G_WKF * HEADS:(G_WKB + 1) * HEADS, :]


def _gate_call(gates, batch, seq):
    tri = jnp.asarray(np.tril(np.ones((CHUNK, CHUNK), np.float32)), BF16)
    p_tri, p_tot, p_id = (jnp.asarray(p, BF16) for p in _gate_routing())
    return pl.pallas_call(
        _gate_kernel,
        grid=(batch,),
        in_specs=[pl.BlockSpec((seq, GATE_LANES), lambda b: (b, 0)),
                  _const_spec((CHUNK, CHUNK)), _const_spec((GATE_LANES, GATE_LANES)),
                  _const_spec((GATE_LANES, GATE_LANES)), _const_spec((GATE_LANES, GATE_LANES))],
        out_specs=[pl.BlockSpec((seq, GATE_LANES), lambda b: (b, 0)),
                   pl.BlockSpec((1, seq // CHUNK, ROW_TABLE_ROWS, CHUNK), lambda b: (b, 0, 0, 0))],
        out_shape=[jax.ShapeDtypeStruct((batch * seq, GATE_LANES), F32),
                   jax.ShapeDtypeStruct((batch, seq // CHUNK, ROW_TABLE_ROWS, CHUNK), F32)],
        compiler_params=_params(1),
        name="mlstm_gates",
    )(gates, tri, p_tri, p_tot, p_id)


def _mlstm_kernel(q_ref, kt_ref, v_ref, og_ref, tab_ref, row_ref, hg_ref, out_ref,
                  u_ref, sf_ref, sb_ref):
    nc = kt_ref.shape[1]
    ones_blk = jnp.ones((CHUNK, DV), BF16)
    zero_k = jnp.zeros((DQK, CHUNK), BF16)
    zero_state = jnp.zeros((DQK, 2 * DV), BF16)
    t_idx = lax.broadcasted_iota(jnp.int32, (CHUNK, CHUNK), 0)
    s_idx = lax.broadcasted_iota(jnp.int32, (CHUNK, CHUNK), 1)
    lower = s_idx <= t_idx
    upper = s_idx >= t_idx

    for h in range(HEADS):
        pair_lanes = slice((h // 2) * 2 * DQK, (h // 2 + 1) * 2 * DQK)
        k_rows = slice(h * DQK, (h + 1) * DQK)
        v_lanes = slice(h * DV, (h + 1) * DV)

        def pad_rows(a, zero, h=h):
            return jnp.concatenate([a, zero] if h % 2 == 0 else [zero, a], axis=0)

        def chunk_rows(c):
            return pl.ds(pl.multiple_of(c * CHUNK, CHUNK), CHUNK)

        def vaug(rows, v_lanes=v_lanes):
            return jnp.concatenate([v_ref[rows, v_lanes], ones_blk], axis=1)

        def row(c, group, h=h):
            return row_ref[0, c, group * HEADS + h:group * HEADS + h + 1, :]

        def col(tab, group, h=h):
            return tab[:, group * HEADS + h:group * HEADS + h + 1]

        def contrib_step(c, carry):
            kt = kt_ref[0, c, k_rows, :].astype(F32)
            kw = jnp.concatenate([(kt * row(c, R_WKF)).astype(BF16),
                                  (kt * row(c, R_WKB)).astype(BF16)], axis=0)
            u_ref[c] = _dot(kw, vaug(chunk_rows(c)))
            return carry

        lax.fori_loop(0, nc, contrib_step, 0, unroll=2)

        def decay(c, group):
            j = group * HEADS + h
            return tab_ref[c * CHUNK:c * CHUNK + 1, j:j + 1]

        state = jnp.zeros((DQK, 2 * DV), F32)
        for c in range(nc):
            sf_ref[c] = state.astype(BF16)
            state = decay(c, G_DECF) * state + u_ref[c, 0:DQK, :]
        state = jnp.zeros((DQK, 2 * DV), F32)
        for c in reversed(range(nc)):
            sb_ref[c] = state.astype(BF16)
            state = decay(c, G_DECB) * state + u_ref[c, DQK:2 * DQK, :]

        def out_step(c, carry):
            rows = chunk_rows(c)
            q = q_ref[rows, pair_lanes]
            s = _dot(q, pad_rows(kt_ref[0, c, k_rows, :], zero_k))
            tab = tab_ref[rows, :]
            d_f = jnp.exp(jnp.where(lower, col(tab, G_CF) + row(c, R_AF), -jnp.inf))
            d_b = jnp.exp(jnp.where(upper, col(tab, G_CB) + row(c, R_AB), -jnp.inf))
            p = jnp.concatenate([(s * d_f).astype(BF16), (s * d_b).astype(BF16)], axis=0)
            nd = _dot(p, vaug(rows))
            states = jnp.concatenate([pad_rows(sf_ref[c], zero_state),
                                      pad_rows(sb_ref[c], zero_state)], axis=1)
            inter = _dot(q, states)
            nd_f = nd[:CHUNK] + col(tab, G_EF) * inter[:, :2 * DV]
            nd_b = nd[CHUNK:] + col(tab, G_SB) * inter[:, 2 * DV:]
            hh = (nd_f[:, :DV] / jnp.maximum(jnp.abs(nd_f[:, DV:]), 1.0)
                  + nd_b[:, :DV] / jnp.maximum(jnp.abs(nd_b[:, DV:]), 1.0))
            hn = hh * lax.rsqrt(jnp.mean(hh * hh, axis=-1, keepdims=True) + EPS)
            gated = hn * hg_ref[:, v_lanes] * og_ref[rows, v_lanes].astype(F32)
            out_ref[rows, v_lanes] = gated.astype(BF16)
            return carry

        lax.fori_loop(0, nc, out_step, 0, unroll=2)


def _mlstm_call(q, kt, v, og, tab, rowtab, head_g, batch, seq):
    hd = HEADS * DV
    nqk = HEADS * DQK
    nc = seq // CHUNK
    seq_blk = lambda n: pl.BlockSpec((seq, n), lambda b: (b, 0))
    return pl.pallas_call(
        _mlstm_kernel,
        grid=(batch,),
        in_specs=[seq_blk(nqk), pl.BlockSpec((1, nc, nqk, CHUNK), lambda b: (b, 0, 0, 0)),
                  seq_blk(hd), seq_blk(hd), seq_blk(GATE_LANES),
                  pl.BlockSpec((1, nc, ROW_TABLE_ROWS, CHUNK), lambda b: (b, 0, 0, 0)),
                  _const_spec((1, hd))],
        out_specs=seq_blk(hd),
        out_shape=jax.ShapeDtypeStruct((batch * seq, hd), BF16),
        scratch_shapes=[pltpu.VMEM((nc, 2 * DQK, 2 * DV), F32),
                        pltpu.VMEM((nc, DQK, 2 * DV), BF16), pltpu.VMEM((nc, DQK, 2 * DV), BF16)],
        compiler_params=_params(1),
        name="mlstm_core",
    )(q, kt, v, og, tab, rowtab, head_g)


def _mlp_kernel(*refs, has_scale, final_norm, ff_chunk):
    refs = list(refs)
    x_ref, a_ref, wo_ref = refs[:3]
    del refs[:3]
    sc_ref = refs.pop(0) if has_scale else None
    g_ref, w1_ref, w2_ref = refs[:3]
    del refs[:3]
    fg_ref = refs.pop(0) if final_norm else None
    (out_ref,) = refs

    r = _dot(a_ref[...], wo_ref[...])
    if has_scale:
        r = r * sc_ref[...]
    x1 = x_ref[...] + r
    u = _rms(x1, g_ref[...]).astype(BF16)
    y = x1
    for c in range(w1_ref.shape[1] // ff_chunk):
        cols = slice(c * ff_chunk, (c + 1) * ff_chunk)
        hid = jnp.square(jnp.maximum(_dot(u, w1_ref[:, cols]), 0.0)).astype(BF16)
        y = y + _dot(hid, w2_ref[cols, :])
    if final_norm:
        y = _rms(y, fg_ref[...])
    out_ref[...] = y


def _mlp_call(x2, a, wo, scale, g, w1, w2, final_g, tm, ff_chunk=1024):
    m, d = x2.shape
    row = lambda n: pl.BlockSpec((tm, n), lambda i: (i, 0))
    args = [x2, a, wo]
    specs = [row(d), row(a.shape[1]), _const_spec(wo.shape)]
    if scale is not None:
        args.append(scale)
        specs.append(_const_spec((1, d)))
    args += [g, w1, w2]
    specs += [_const_spec((1, d)), _const_spec(w1.shape), _const_spec(w2.shape)]
    if final_g is not None:
        args.append(final_g)
        specs.append(_const_spec((1, d)))
    body = functools.partial(_mlp_kernel, has_scale=scale is not None,
                             final_norm=final_g is not None, ff_chunk=ff_chunk)
    return pl.pallas_call(
        body,
        grid=(m // tm,),
        in_specs=specs,
        out_specs=row(d),
        out_shape=jax.ShapeDtypeStruct((m, d), F32),
        compiler_params=_params(1),
        name="mlp",
    )(*args)


def _pool_kernel(x_ref, g_ref, win_ref, wgrp_ref, out_ref, a_scr, *, proj_rows, pool_rows):
    seq, d = x_ref.shape
    gdim = d // len(POOL_WINDOWS)
    a_scr[0:POOL_HALO, :] = jnp.zeros((POOL_HALO, d), F32)
    a_scr[POOL_HALO + seq:, :] = jnp.zeros((POOL_HALO, d), F32)
    for t in range(seq // proj_rows):
        rows = slice(t * proj_rows, (t + 1) * proj_rows)
        u = _rms(x_ref[rows, :], g_ref[...]).astype(BF16)
        a_scr[POOL_HALO + t * proj_rows:POOL_HALO + (t + 1) * proj_rows, :] = _dot(u, win_ref[...])

    n = pool_rows + 2 * POOL_HALO
    for t in range(seq // pool_rows):
        t0 = t * pool_rows
        pos = t0 + lax.broadcasted_iota(jnp.int32, (pool_rows, gdim), 0)
        for gi, w in enumerate(POOL_WINDOWS):
            cols = slice(gi * gdim, (gi + 1) * gdim)
            xw = a_scr[t0:t0 + n, cols]
            s = xw + pltpu.roll(xw, 1, axis=0)
            step = 1
            while 2 * step < w:
                s = pltpu.roll(s, step, axis=0) + pltpu.roll(s, n - step, axis=0)
                step *= 2
            count = (jnp.minimum(pos + w // 2, seq) - jnp.maximum(pos - w // 2, 0)).astype(F32)
            centre = slice(POOL_HALO, POOL_HALO + pool_rows)
            pooled = s[centre, :] / count - xw[centre, :]
            mixed = _dot(pooled.astype(BF16), wgrp_ref[gi])
            out_ref[t0:t0 + pool_rows, cols] = mixed.astype(BF16)


def _pool_call(x2, g, w_in, w_group, batch, seq, proj_rows=512, pool_rows=256):
    d = x2.shape[1]
    seq_blk = pl.BlockSpec((seq, d), lambda b: (b, 0))
    body = functools.partial(_pool_kernel, proj_rows=proj_rows, pool_rows=pool_rows)
    return pl.pallas_call(
        body,
        grid=(batch,),
        in_specs=[seq_blk, _const_spec((1, d)), _const_spec(w_in.shape), _const_spec(w_group.shape)],
        out_specs=seq_blk,
        out_shape=jax.ShapeDtypeStruct((batch * seq, d), BF16),
        scratch_shapes=[pltpu.VMEM((seq + 2 * POOL_HALO, d), F32)],
        compiler_params=_params(1),
        name="pool_mixer",
    )(x2, g, w_in, w_group)


def kernel(x, mix_norm_g, mlp_norm_g, mlstm_w_in, mlstm_gate_b, mlstm_head_g, mlstm_w_out,
           pool_w_in, pool_w_group, pool_w_out, pool_scale, mlp_w1, mlp_w2, final_norm_g):
    batch, seq, d = x.shape
    assert seq % CHUNK == 0 and d == HEADS * DV
    m = batch * seq
    tm = 512
    x2 = x.reshape(m, d)
    vec = lambda a: a.reshape(1, -1).astype(F32)

    w_in = mlstm_w_in[0]
    nqk = HEADS * DQK
    wq = (w_in[:, :nqk] * (DQK ** -0.5)).astype(BF16)
    wkt = w_in[:, nqk:2 * nqk].T.astype(BF16)
    wv = w_in[:, 2 * nqk:2 * nqk + d].astype(BF16)
    wo = w_in[:, 2 * nqk + d:2 * nqk + 2 * d].astype(BF16)
    n_gate = 4 * HEADS
    wg = jnp.pad(w_in[:, 2 * nqk + 2 * d:], ((0, 0), (0, GATE_LANES - n_gate))).astype(BF16)
    gb = jnp.pad(mlstm_gate_b[0].astype(F32), (0, GATE_LANES - n_gate)).reshape(1, GATE_LANES)

    q, kt, v, og, gates = _proj_call(x2, vec(mix_norm_g[0]), wq, wkt, wv, wo, wg, gb, tm, batch, seq)
    tab, rowtab = _gate_call(gates, batch, seq)
    hmix = _mlstm_call(q, kt, v, og, tab, rowtab, vec(mlstm_head_g[0]), batch, seq)
    x2 = _mlp_call(x2, hmix, mlstm_w_out[0].astype(BF16), None, vec(mlp_norm_g[0]),
                   mlp_w1[0].astype(BF16), mlp_w2[0].astype(BF16), None, tm)

    mixed = _pool_call(x2, vec(mix_norm_g[1]), pool_w_in[0].astype(BF16),
                       pool_w_group[0].astype(BF16), batch, seq)
    x2 = _mlp_call(x2, mixed, pool_w_out[0].astype(BF16), vec(pool_scale[0]), vec(mlp_norm_g[1]),
                   mlp_w1[1].astype(BF16), mlp_w2[1].astype(BF16), vec(final_norm_g), tm)
    return x2.reshape(batch, seq, d)
```

```python
import functools

import numpy as np
import jax
import jax.numpy as jnp
from jax import lax
from jax.experimental import pallas as pl
from jax.experimental.pallas import tpu as pltpu

F32 = jnp.float32
BF16 = jnp.bfloat16

HEADS = 8
DQK = 64
DV = 128
CHUNK = 256
GATE_SOFTCAP = 15.0
POOL_WINDOWS = (2, 4, 8, 16)
POOL_HALO = 8
EPS = 1e-6
LOG2_E = 1.4426950408889634
GATE_LANES = 128
VMEM_LIMIT = 56 * 1024 * 1024

G_CF, G_CB, G_TOTB, G_AF, G_AB, G_WKF, G_WKB, G_DECF, G_DECB = range(9)
R_AF, R_AB, R_WKF, R_WKB = range(4)
ROW_TABLE_ROWS = 4 * HEADS


def _rms(x, g):
    ms = jnp.mean(x * x, axis=-1, keepdims=True)
    return x * lax.rsqrt(ms + EPS) * g


def _dot(a, b):
    return jnp.dot(a, b, preferred_element_type=F32)


def _dot_tn(a, b):
    return lax.dot_general(a, b, (((0,), (0,)), ((), ())), preferred_element_type=F32)


def _dot_nt(a, b):
    return lax.dot_general(a, b, (((1,), (1,)), ((), ())), preferred_element_type=F32)


def _split3(x):
    hi = x.astype(BF16)
    r1 = x - hi.astype(F32)
    mid = r1.astype(BF16)
    lo = (r1 - mid.astype(F32)).astype(BF16)
    return hi, mid, lo


def _dot3(parts, m):
    return _dot(parts[0], m) + _dot(parts[1], m) + _dot(parts[2], m)


def _dot3_left(m, parts):
    return _dot(m, parts[0]) + _dot(m, parts[1]) + _dot(m, parts[2])


def _const_spec(shape):
    zeros = (0,) * len(shape)
    return pl.BlockSpec(shape, lambda *_: zeros, pipeline_mode=pl.Buffered(1))


def _params(n_axes):
    return pltpu.CompilerParams(dimension_semantics=("arbitrary",) * n_axes,
                                vmem_limit_bytes=VMEM_LIMIT)


def _proj_kernel(x_ref, g_ref, w_ref, wg_ref, gb_ref, hg_ref,
                 q_ref, kt_ref, v_ref, og_ref, gate_ref):
    nqk = HEADS * DQK
    hd = HEADS * DV
    u = _rms(x_ref[...], g_ref[...]).astype(BF16)
    q_ref[...] = (_dot(u, w_ref[:, 0:nqk]) * (DQK ** -0.5)).astype(BF16)
    kt = lax.dot_general(w_ref[:, nqk:2 * nqk], u, (((0,), (1,)), ((), ())),
                         preferred_element_type=F32).astype(BF16)
    for c in range(kt_ref.shape[1]):
        kt_ref[0, c] = kt[:, c * CHUNK:(c + 1) * CHUNK]
    v_ref[...] = _dot(u, w_ref[:, 2 * nqk:2 * nqk + hd]).astype(BF16)
    o = _dot(u, w_ref[:, 2 * nqk + hd:2 * nqk + 2 * hd])
    og_ref[...] = (jax.nn.sigmoid(o) * hg_ref[...]).astype(BF16)
    g = _dot(u, wg_ref[...]) + gb_ref[...]
    gate_ref[...] = GATE_SOFTCAP * jnp.tanh(g / GATE_SOFTCAP)


def _proj_call(x2, g, w, wg, gb, head_g, tm, batch, seq):
    m, d = x2.shape
    hd = HEADS * DV
    nqk = HEADS * DQK
    assert tm % CHUNK == 0 and seq % tm == 0
    tiles_per_seq = seq // tm
    row = lambda n: pl.BlockSpec((tm, n), lambda i: (i, 0))
    kt_spec = pl.BlockSpec((1, tm // CHUNK, nqk, CHUNK),
                           lambda i: (i // tiles_per_seq, i % tiles_per_seq, 0, 0))
    return pl.pallas_call(
        _proj_kernel,
        grid=(m // tm,),
        in_specs=[row(d), _const_spec((1, d)), _const_spec(w.shape), _const_spec(wg.shape),
                  _const_spec((1, GATE_LANES)), _const_spec((1, hd))],
        out_specs=[row(nqk), kt_spec, row(hd), row(hd), row(GATE_LANES)],
        out_shape=[jax.ShapeDtypeStruct((m, nqk), BF16),
                   jax.ShapeDtypeStruct((batch, seq // CHUNK, nqk, CHUNK), BF16),
                   jax.ShapeDtypeStruct((m, hd), BF16),
                   jax.ShapeDtypeStruct((m, hd), BF16),
                   jax.ShapeDtypeStruct((m, GATE_LANES), F32)],
        compiler_params=_params(1),
        name="mlstm_proj",
    )(x2, g, w, wg, gb, head_g)


def _gate_routing():
    n = GATE_LANES
    p_tri = np.zeros((n, n), np.float32)
    p_tot = np.zeros((n, n), np.float32)
    p_id = np.zeros((n, n), np.float32)
    i_f, lf_f, i_b, lf_b = 0, 1, 2, 3

    def put(p, src, dst, sign):
        for h in range(HEADS):
            p[src * HEADS + h, dst * HEADS + h] += sign

    put(p_tri, lf_f, G_CF, 1.0)
    for dst in (G_WKF, G_AF):
        put(p_tri, lf_f, dst, -1.0)
        put(p_id, i_f, dst, 1.0)
    put(p_tri, lf_b, G_CB, -1.0)
    put(p_id, lf_b, G_CB, 1.0)
    for dst in (G_WKB, G_AB):
        put(p_tri, lf_b, dst, 1.0)
        put(p_id, i_b, dst, 1.0)
        put(p_id, lf_b, dst, -1.0)
    put(p_tot, lf_f, G_WKF, 1.0)
    put(p_tot, lf_f, G_DECF, 1.0)
    put(p_tot, lf_b, G_DECB, 1.0)
    put(p_tot, lf_b, G_TOTB, 1.0)
    return p_tri, p_tot, p_id


def _gate_kernel(gate_ref, tri_ref, ptri_ref, ptot_ref, pid_ref, tab_ref, row_ref):
    seq = gate_ref.shape[0]
    lane = lax.broadcasted_iota(jnp.int32, (CHUNK, GATE_LANES), 1)
    grp = lane // HEADS
    is_forget = (grp == 1) | (grp == 3)
    is_exp = (grp >= G_WKF) & (grp <= G_DECB)
    for c in range(seq // CHUNK):
        rows = slice(c * CHUNK, (c + 1) * CHUNK)
        z = gate_ref[rows, :]
        log_sig = jnp.minimum(z, 0.0) - jnp.log1p(jnp.exp(-jnp.abs(z)))
        z = jnp.where(is_forget, log_sig, z)
        zs = _split3(z)
        y_tri = _dot3(zs, ptri_ref[...])
        y_tot = _dot3(zs, ptot_ref[...])
        y_id = _dot3(zs, pid_ref[...])
        pre = (_dot3_left(tri_ref[...], _split3(y_tri))
               + jnp.sum(y_tot, axis=0, keepdims=True) + y_id)
        tab = jnp.where(is_exp, jnp.exp(pre), pre * LOG2_E)
        tab_ref[rows, :] = tab
        row_ref[0, c] = tab.T[G_AF * HEADS:(G_WKB + 1) * HEADS, :]


def _gate_call(gates, batch, seq):
    tri = jnp.asarray(np.tril(np.ones((CHUNK, CHUNK), np.float32)), BF16)
    p_tri, p_tot, p_id = (jnp.asarray(p, BF16) for p in _gate_routing())
    return pl.pallas_call(
        _gate_kernel,
        grid=(batch,),
        in_specs=[pl.BlockSpec((seq, GATE_LANES), lambda b: (b, 0)),
                  _const_spec((CHUNK, CHUNK)), _const_spec((GATE_LANES, GATE_LANES)),
                  _const_spec((GATE_LANES, GATE_LANES)), _const_spec((GATE_LANES, GATE_LANES))],
        out_specs=[pl.BlockSpec((seq, GATE_LANES), lambda b: (b, 0)),
                   pl.BlockSpec((1, seq // CHUNK, ROW_TABLE_ROWS, CHUNK), lambda b: (b, 0, 0, 0))],
        out_shape=[jax.ShapeDtypeStruct((batch * seq, GATE_LANES), F32),
                   jax.ShapeDtypeStruct((batch, seq // CHUNK, ROW_TABLE_ROWS, CHUNK), F32)],
        compiler_params=_params(1),
        name="mlstm_gates",
    )(gates, tri, p_tri, p_tot, p_id)


def _mlstm_kernel(q_ref, kt_ref, v_ref, og_ref, tab_ref, row_ref, out_ref,
                  u_ref, sf_ref, sb_ref):
    nc = kt_ref.shape[1]
    ones_blk = jnp.ones((CHUNK, DV), BF16)
    zero_k = jnp.zeros((DQK, CHUNK), BF16)
    zero_state = jnp.zeros((DQK, 2 * DV), BF16)
    zero_q = jnp.zeros((CHUNK, 2 * DQK), BF16)
    t_idx = lax.broadcasted_iota(jnp.int32, (CHUNK, CHUNK), 0)
    s_idx = lax.broadcasted_iota(jnp.int32, (CHUNK, CHUNK), 1)
    lower = s_idx <= t_idx
    upper = s_idx >= t_idx

    for h in range(HEADS):
        pair_lanes = slice((h // 2) * 2 * DQK, (h // 2 + 1) * 2 * DQK)
        k_rows = slice(h * DQK, (h + 1) * DQK)
        v_lanes = slice(h * DV, (h + 1) * DV)

        def pad_rows(a, zero, h=h):
            return jnp.concatenate([a, zero] if h % 2 == 0 else [zero, a], axis=0)

        def chunk_rows(c):
            return pl.ds(pl.multiple_of(c * CHUNK, CHUNK), CHUNK)

        def vaug(rows, v_lanes=v_lanes):
            return jnp.concatenate([v_ref[rows, v_lanes], ones_blk], axis=1)

        def row(c, group, h=h):
            return row_ref[0, c, group * HEADS + h:group * HEADS + h + 1, :]

        def col(tab, group, h=h):
            return tab[:, group * HEADS + h:group * HEADS + h + 1]

        def contrib_step(c, carry):
            kt = kt_ref[0, c, k_rows, :].astype(F32)
            kw = jnp.concatenate([(kt * row(c, R_WKF)).astype(BF16),
                                  (kt * row(c, R_WKB)).astype(BF16)], axis=0)
            u_ref[c] = _dot(kw, vaug(chunk_rows(c)))
            return carry

        lax.fori_loop(0, nc, contrib_step, 0, unroll=4)

        def decay(c, group):
            j = group * HEADS + h
            return tab_ref[c * CHUNK:c * CHUNK + 1, j:j + 1]

        state = jnp.zeros((DQK, 2 * DV), F32)
        for c in range(nc):
            sf_ref[c] = state.astype(BF16)
            state = decay(c, G_DECF) * state + u_ref[c, 0:DQK, :]
        state = jnp.zeros((DQK, 2 * DV), F32)
        for c in reversed(range(nc)):
            sb_ref[c] = state.astype(BF16)
            state = decay(c, G_DECB) * state + u_ref[c, DQK:2 * DQK, :]

        def out_step(c, carry):
            rows = chunk_rows(c)
            q = q_ref[rows, pair_lanes]
            s = _dot(q, pad_rows(kt_ref[0, c, k_rows, :], zero_k))
            tab = tab_ref[rows, :]
            c_f = jnp.broadcast_to(col(tab, G_CF), (CHUNK, CHUNK))
            c_b = jnp.broadcast_to(col(tab, G_CB), (CHUNK, CHUNK))
            d_f = jnp.exp2(jnp.where(lower, c_f + row(c, R_AF), -jnp.inf))
            d_b = jnp.exp2(jnp.where(upper, c_b + row(c, R_AB), -jnp.inf))
            qf32 = q.astype(F32)
            q_f = (qf32 * jnp.exp2(c_f[:, :2 * DQK])).astype(BF16)
            q_b = (qf32 * jnp.exp2(c_b[:, :2 * DQK] + col(tab, G_TOTB)[0:1, :])).astype(BF16)
            lhs = jnp.concatenate(
                [jnp.concatenate([(s * d_f).astype(BF16), q_f, zero_q], axis=1),
                 jnp.concatenate([(s * d_b).astype(BF16), zero_q, q_b], axis=1)], axis=0)
            rhs = jnp.concatenate([vaug(rows), pad_rows(sf_ref[c], zero_state),
                                   pad_rows(sb_ref[c], zero_state)], axis=0)
            nd = _dot(lhs, rhs)
            nd_f = nd[:CHUNK]
            nd_b = nd[CHUNK:]
            hh = (nd_f[:, :DV] / jnp.maximum(jnp.abs(nd_f[:, DV:]), 1.0)
                  + nd_b[:, :DV] / jnp.maximum(jnp.abs(nd_b[:, DV:]), 1.0))
            hn = hh * lax.rsqrt(jnp.mean(hh * hh, axis=-1, keepdims=True) + EPS)
            gated = hn * og_ref[rows, v_lanes].astype(F32)
            out_ref[rows, v_lanes] = gated.astype(BF16)
            return carry

        lax.fori_loop(0, nc, out_step, 0, unroll=4)


def _mlstm_call(q, kt, v, og, tab, rowtab, batch, seq):
    hd = HEADS * DV
    nqk = HEADS * DQK
    nc = seq // CHUNK
    seq_blk = lambda n: pl.BlockSpec((seq, n), lambda b: (b, 0))
    return pl.pallas_call(
        _mlstm_kernel,
        grid=(batch,),
        in_specs=[seq_blk(nqk), pl.BlockSpec((1, nc, nqk, CHUNK), lambda b: (b, 0, 0, 0)),
                  seq_blk(hd), seq_blk(hd), seq_blk(GATE_LANES),
                  pl.BlockSpec((1, nc, ROW_TABLE_ROWS, CHUNK), lambda b: (b, 0, 0, 0))],
        out_specs=seq_blk(hd),
        out_shape=jax.ShapeDtypeStruct((batch * seq, hd), BF16),
        scratch_shapes=[pltpu.VMEM((nc, 2 * DQK, 2 * DV), F32),
                        pltpu.VMEM((nc, DQK, 2 * DV), BF16), pltpu.VMEM((nc, DQK, 2 * DV), BF16)],
        compiler_params=_params(1),
        name="mlstm_core",
    )(q, kt, v, og, tab, rowtab)


def _mlp_kernel(*refs, has_scale, final_norm, ff_chunk):
    refs = list(refs)
    x_ref, a_ref, wo_ref = refs[:3]
    del refs[:3]
    sc_ref = refs.pop(0) if has_scale else None
    g_ref, w1_ref, w2_ref = refs[:3]
    del refs[:3]
    fg_ref = refs.pop(0) if final_norm else None
    (out_ref,) = refs

    r = _dot(a_ref[...], wo_ref[...])
    if has_scale:
        r = r * sc_ref[...]
    x1 = x_ref[...] + r
    u = _rms(x1, g_ref[...]).astype(BF16)
    y = x1
    for c in range(w1_ref.shape[1] // ff_chunk):
        cols = slice(c * ff_chunk, (c + 1) * ff_chunk)
        hid = jnp.square(jnp.maximum(_dot(u, w1_ref[:, cols]), 0.0)).astype(BF16)
        y = y + _dot(hid, w2_ref[cols, :])
    if final_norm:
        y = _rms(y, fg_ref[...])
    out_ref[...] = y


def _layer_spec(shape, layer):
    zeros = (0,) * (len(shape) - 1)
    return pl.BlockSpec((None,) + tuple(shape[1:]), lambda *_: (layer,) + zeros,
                        pipeline_mode=pl.Buffered(1))


def _mlp_call(x2, a, wo, scale, g, w1, w2, layer, final_g, tm, ff_chunk=1024):
    m, d = x2.shape
    row = lambda n: pl.BlockSpec((tm, n), lambda i: (i, 0))
    args = [x2, a, wo]
    specs = [row(d), row(a.shape[1]), _const_spec(wo.shape)]
    if scale is not None:
        args.append(scale)
        specs.append(_const_spec((1, d)))
    args += [g, w1, w2]
    specs += [_const_spec((1, d)), _layer_spec(w1.shape, layer), _layer_spec(w2.shape, layer)]
    if final_g is not None:
        args.append(final_g)
        specs.append(_const_spec((1, d)))
    body = functools.partial(_mlp_kernel, has_scale=scale is not None,
                             final_norm=final_g is not None, ff_chunk=ff_chunk)
    return pl.pallas_call(
        body,
        grid=(m // tm,),
        in_specs=specs,
        out_specs=row(d),
        out_shape=jax.ShapeDtypeStruct((m, d), F32),
        compiler_params=_params(1),
        name="mlp",
    )(*args)


def _pool_kernel(x_ref, g_ref, win_ref, wgrp_ref, out_ref, a_scr, *, proj_rows, pool_rows):
    seq, d = x_ref.shape
    gdim = d // len(POOL_WINDOWS)
    a_scr[0:POOL_HALO, :] = jnp.zeros((POOL_HALO, d), F32)
    a_scr[POOL_HALO + seq:, :] = jnp.zeros((POOL_HALO, d), F32)
    for t in range(seq // proj_rows):
        rows = slice(t * proj_rows, (t + 1) * proj_rows)
        u = _rms(x_ref[rows, :], g_ref[...]).astype(BF16)
        a_scr[POOL_HALO + t * proj_rows:POOL_HALO + (t + 1) * proj_rows, :] = _dot(u, win_ref[...])

    n = pool_rows + 2 * POOL_HALO
    for t in range(seq // pool_rows):
        t0 = t * pool_rows
        pos = t0 + lax.broadcasted_iota(jnp.int32, (pool_rows, gdim), 0)
        for gi, w in enumerate(POOL_WINDOWS):
            cols = slice(gi * gdim, (gi + 1) * gdim)
            xw = a_scr[t0:t0 + n, cols]
            s = xw + pltpu.roll(xw, 1, axis=0)
            step = 1
            while 2 * step < w:
                s = pltpu.roll(s, step, axis=0) + pltpu.roll(s, n - step, axis=0)
                step *= 2
            count = (jnp.minimum(pos + w // 2, seq) - jnp.maximum(pos - w // 2, 0)).astype(F32)
            centre = slice(POOL_HALO, POOL_HALO + pool_rows)
            pooled = s[centre, :] / count - xw[centre, :]
            mixed = _dot(pooled.astype(BF16), wgrp_ref[gi])
            out_ref[t0:t0 + pool_rows, cols] = mixed.astype(BF16)


def _pool_call(x2, g, w_in, w_group, batch, seq, proj_rows=512, pool_rows=256):
    d = x2.shape[1]
    seq_blk = pl.BlockSpec((seq, d), lambda b: (b, 0))
    body = functools.partial(_pool_kernel, proj_rows=proj_rows, pool_rows=pool_rows)
    return pl.pallas_call(
        body,
        grid=(batch,),
        in_specs=[seq_blk, _const_spec((1, d)), _const_spec(w_in.shape), _const_spec(w_group.shape)],
        out_specs=seq_blk,
        out_shape=jax.ShapeDtypeStruct((batch * seq, d), BF16),
        scratch_shapes=[pltpu.VMEM((seq + 2 * POOL_HALO, d), F32)],
        compiler_params=_params(1),
        name="pool_mixer",
    )(x2, g, w_in, w_group)


def kernel(x, mix_norm_g, mlp_norm_g, mlstm_w_in, mlstm_gate_b, mlstm_head_g, mlstm_w_out,
           pool_w_in, pool_w_group, pool_w_out, pool_scale, mlp_w1, mlp_w2, final_norm_g):
    batch, seq, d = x.shape
    assert seq % CHUNK == 0 and d == HEADS * DV
    m = batch * seq
    tm = 512
    x2 = x.reshape(m, d)
    vec = lambda a: a.reshape(1, -1).astype(F32)

    w_in = mlstm_w_in[0]
    n_main = 2 * HEADS * DQK + 2 * d
    n_gate = 4 * HEADS
    w_main = w_in.astype(BF16)
    wg = jnp.pad(w_in[:, n_main:], ((0, 0), (0, GATE_LANES - n_gate))).astype(BF16)
    gb = jnp.pad(mlstm_gate_b[0].astype(F32), (0, GATE_LANES - n_gate)).reshape(1, GATE_LANES)
    w1 = mlp_w1.astype(BF16)
    w2 = mlp_w2.astype(BF16)

    q, kt, v, og, gates = _proj_call(x2, vec(mix_norm_g[0]), w_main, wg, gb, vec(mlstm_head_g[0]),
                                     tm, batch, seq)
    tab, rowtab = _gate_call(gates, batch, seq)
    hmix = _mlstm_call(q, kt, v, og, tab, rowtab, batch, seq)
    x2 = _mlp_call(x2, hmix, mlstm_w_out[0].astype(BF16), None, vec(mlp_norm_g[0]),
                   w1, w2, 0, None, tm)

    mixed = _pool_call(x2, vec(mix_norm_g[1]), pool_w_in[0].astype(BF16),
                       pool_w_group[0].astype(BF16), batch, seq)
    x2 = _mlp_call(x2, mixed, pool_w_out[0].astype(BF16), vec(pool_scale[0]), vec(mlp_norm_g[1]),
                   w1, w2, 1, vec(final_norm_g), tm)
    return x2.reshape(batch, seq, d)
```

```python
import functools

import numpy as np
import jax
import jax.numpy as jnp
from jax import lax
from jax.experimental import pallas as pl
from jax.experimental.pallas import tpu as pltpu

F32 = jnp.float32
BF16 = jnp.bfloat16

HEADS = 8
DQK = 64
DV = 128
CHUNK = 256
GATE_SOFTCAP = 15.0
POOL_WINDOWS = (2, 4, 8, 16)
POOL_HALO = 8
EPS = 1e-6
LOG2_E = 1.4426950408889634
GATE_LANES = 128
VMEM_LIMIT = 56 * 1024 * 1024

N_GATES = 4 * HEADS
G_CF, G_CB = range(2)
R_AF, R_AB, R_TOTB, R_WKF, R_WKB, R_DECF, R_DECB = range(7)
ROW_TABLE_ROWS = 7 * HEADS


def _rms(x, g):
    ms = jnp.mean(x * x, axis=-1, keepdims=True)
    return x * lax.rsqrt(ms + EPS) * g


def _dot(a, b):
    return jnp.dot(a, b, preferred_element_type=F32)


def _dot_tn(a, b):
    return lax.dot_general(a, b, (((0,), (0,)), ((), ())), preferred_element_type=F32)


def _dot_nt(a, b):
    return lax.dot_general(a, b, (((1,), (1,)), ((), ())), preferred_element_type=F32)


def _split3(x):
    hi = x.astype(BF16)
    r1 = x - hi.astype(F32)
    mid = r1.astype(BF16)
    lo = (r1 - mid.astype(F32)).astype(BF16)
    return hi, mid, lo


def _dot3(parts, m):
    return _dot(parts[0], m) + _dot(parts[1], m) + _dot(parts[2], m)


def _dot3_left(m, parts):
    return _dot(m, parts[0]) + _dot(m, parts[1]) + _dot(m, parts[2])


def _const_spec(shape):
    zeros = (0,) * len(shape)
    return pl.BlockSpec(shape, lambda *_: zeros, pipeline_mode=pl.Buffered(1))


def _params(n_axes):
    return pltpu.CompilerParams(dimension_semantics=("arbitrary",) * n_axes,
                                vmem_limit_bytes=VMEM_LIMIT)


def _proj_kernel(x_ref, g_ref, w_ref, wkg_ref, gb_ref, hg_ref,
                 q_ref, kt_ref, v_ref, og_ref, gate_ref):
    nqk = HEADS * DQK
    hd = HEADS * DV
    u = _rms(x_ref[...], g_ref[...]).astype(BF16)
    q_ref[...] = (_dot(u, w_ref[:, 0:nqk]) * (DQK ** -0.5)).astype(BF16)
    ktg = lax.dot_general(wkg_ref[...], u, (((0,), (1,)), ((), ())), preferred_element_type=F32)
    kt = ktg[0:nqk].astype(BF16)
    for c in range(kt_ref.shape[1]):
        kt_ref[0, c] = kt[:, c * CHUNK:(c + 1) * CHUNK]
    g = ktg[nqk:nqk + N_GATES] + gb_ref[...]
    gate_ref[...] = GATE_SOFTCAP * jnp.tanh(g / GATE_SOFTCAP)
    v_ref[...] = _dot(u, w_ref[:, 2 * nqk:2 * nqk + hd]).astype(BF16)
    o = _dot(u, w_ref[:, 2 * nqk + hd:2 * nqk + 2 * hd])
    og_ref[...] = (jax.nn.sigmoid(o) * hg_ref[...]).astype(BF16)


def _proj_call(x2, g, w, wkg, gb, head_g, tm, batch, seq):
    m, d = x2.shape
    hd = HEADS * DV
    nqk = HEADS * DQK
    assert tm % CHUNK == 0 and seq % tm == 0
    tiles_per_seq = seq // tm
    row = lambda n: pl.BlockSpec((tm, n), lambda i: (i, 0))
    kt_spec = pl.BlockSpec((1, tm // CHUNK, nqk, CHUNK),
                           lambda i: (i // tiles_per_seq, i % tiles_per_seq, 0, 0))
    return pl.pallas_call(
        _proj_kernel,
        grid=(m // tm,),
        in_specs=[row(d), _const_spec((1, d)), _const_spec(w.shape), _const_spec(wkg.shape),
                  _const_spec((N_GATES, 1)), _const_spec((1, hd))],
        out_specs=[row(nqk), kt_spec, row(hd), row(hd),
                   pl.BlockSpec((N_GATES, tm), lambda i: (0, i))],
        out_shape=[jax.ShapeDtypeStruct((m, nqk), BF16),
                   jax.ShapeDtypeStruct((batch, seq // CHUNK, nqk, CHUNK), BF16),
                   jax.ShapeDtypeStruct((m, hd), BF16),
                   jax.ShapeDtypeStruct((m, hd), BF16),
                   jax.ShapeDtypeStruct((N_GATES, m), F32)],
        compiler_params=_params(1),
        name="mlstm_proj",
    )(x2, g, w, wkg, gb, head_g)


def _log_sigmoid(z):
    return jnp.minimum(z, 0.0) - jnp.log1p(jnp.exp(-jnp.abs(z)))


def _gate_kernel(gate_ref, triu_ref, ones_ref, tab_ref, row_ref):
    nc = row_ref.shape[1]
    g = gate_ref[...]
    i_f = g[0:HEADS]
    lf_f = _log_sigmoid(g[HEADS:2 * HEADS])
    i_b = g[2 * HEADS:3 * HEADS]
    lf_b = _log_sigmoid(g[3 * HEADS:4 * HEADS])

    def chunk(a, c):
        return a[:, c * CHUNK:(c + 1) * CHUNK]

    stacked = jnp.concatenate([chunk(lf_f, c) for c in range(nc)]
                              + [chunk(lf_b, c) for c in range(nc)], axis=0)
    parts = _split3(stacked)
    cum = _dot3(parts, triu_ref[...])
    tot = _dot3(parts, ones_ref[...])
    pad = jnp.zeros((GATE_LANES - 2 * HEADS, CHUNK), F32)
    for c in range(nc):
        fwd = slice(c * HEADS, (c + 1) * HEADS)
        bwd = slice((nc + c) * HEADS, (nc + c + 1) * HEADS)
        b_f, tot_f = cum[fwd], tot[fwd]
        e_b, tot_b = cum[bwd] - chunk(lf_b, c), tot[bwd]
        a_f = chunk(i_f, c) - b_f
        a_b = chunk(i_b, c) + e_b
        groups = {R_AF: a_f * LOG2_E, R_AB: a_b * LOG2_E, R_TOTB: tot_b * LOG2_E,
                  R_WKF: jnp.exp(tot_f + a_f), R_WKB: jnp.exp(a_b),
                  R_DECF: jnp.exp(tot_f), R_DECB: jnp.exp(tot_b)}
        for r, val in groups.items():
            row_ref[0, c, r * HEADS:(r + 1) * HEADS, :] = val
        cols = jnp.concatenate([b_f * LOG2_E, -e_b * LOG2_E, pad], axis=0)
        tab_ref[c * CHUNK:(c + 1) * CHUNK, :] = cols.T


def _gate_call(gates_t, batch, seq):
    triu = jnp.asarray(np.triu(np.ones((CHUNK, CHUNK), np.float32)), BF16)
    ones = jnp.ones((CHUNK, CHUNK), BF16)
    return pl.pallas_call(
        _gate_kernel,
        grid=(batch,),
        in_specs=[pl.BlockSpec((N_GATES, seq), lambda b: (0, b)),
                  _const_spec((CHUNK, CHUNK)), _const_spec((CHUNK, CHUNK))],
        out_specs=[pl.BlockSpec((seq, GATE_LANES), lambda b: (b, 0)),
                   pl.BlockSpec((1, seq // CHUNK, ROW_TABLE_ROWS, CHUNK), lambda b: (b, 0, 0, 0))],
        out_shape=[jax.ShapeDtypeStruct((batch * seq, GATE_LANES), F32),
                   jax.ShapeDtypeStruct((batch, seq // CHUNK, ROW_TABLE_ROWS, CHUNK), F32)],
        compiler_params=_params(1),
        name="mlstm_gates",
    )(gates_t, triu, ones)


def _mlstm_kernel(q_ref, kt_ref, v_ref, og_ref, tab_ref, row_ref, out_ref,
                  u_ref, sf_ref, sb_ref):
    nc = kt_ref.shape[1]
    ones_blk = jnp.ones((CHUNK, DV), BF16)
    zero_k = jnp.zeros((DQK, CHUNK), BF16)
    zero_state = jnp.zeros((DQK, 2 * DV), BF16)
    zero_q = jnp.zeros((CHUNK, 2 * DQK), BF16)
    t_idx = lax.broadcasted_iota(jnp.int32, (CHUNK, CHUNK), 0)
    s_idx = lax.broadcasted_iota(jnp.int32, (CHUNK, CHUNK), 1)
    lower = s_idx <= t_idx
    upper = s_idx >= t_idx

    for h in range(HEADS):
        pair_lanes = slice((h // 2) * 2 * DQK, (h // 2 + 1) * 2 * DQK)
        k_rows = slice(h * DQK, (h + 1) * DQK)
        v_lanes = slice(h * DV, (h + 1) * DV)

        def pad_rows(a, zero, h=h):
            return jnp.concatenate([a, zero] if h % 2 == 0 else [zero, a], axis=0)

        def chunk_rows(c):
            return pl.ds(pl.multiple_of(c * CHUNK, CHUNK), CHUNK)

        def vaug(rows, v_lanes=v_lanes):
            return jnp.concatenate([v_ref[rows, v_lanes], ones_blk], axis=1)

        def row(c, group, h=h):
            return row_ref[0, c, group * HEADS + h:group * HEADS + h + 1, :]

        def col(tab, group, h=h):
            return tab[:, group * HEADS + h:group * HEADS + h + 1]

        def contrib_step(c, carry):
            kt = kt_ref[0, c, k_rows, :].astype(F32)
            kw = jnp.concatenate([(kt * row(c, R_WKF)).astype(BF16),
                                  (kt * row(c, R_WKB)).astype(BF16)], axis=0)
            u_ref[c] = _dot(kw, vaug(chunk_rows(c)))
            return carry

        lax.fori_loop(0, nc, contrib_step, 0, unroll=4)

        def chunk_scalar(c, group, h=h):
            return row_ref[0, c, group * HEADS + h:group * HEADS + h + 1, 0:1]

        state = jnp.zeros((DQK, 2 * DV), F32)
        for c in range(nc):
            sf_ref[c] = state.astype(BF16)
            state = chunk_scalar(c, R_DECF) * state + u_ref[c, 0:DQK, :]
        state = jnp.zeros((DQK, 2 * DV), F32)
        for c in reversed(range(nc)):
            sb_ref[c] = state.astype(BF16)
            state = chunk_scalar(c, R_DECB) * state + u_ref[c, DQK:2 * DQK, :]

        def out_step(c, carry):
            rows = chunk_rows(c)
            q = q_ref[rows, pair_lanes]
            s = _dot(q, pad_rows(kt_ref[0, c, k_rows, :], zero_k))
            tab = tab_ref[rows, :]
            c_f = jnp.broadcast_to(col(tab, G_CF), (CHUNK, CHUNK))
            c_b = jnp.broadcast_to(col(tab, G_CB), (CHUNK, CHUNK))
            d_f = jnp.exp2(jnp.where(lower, c_f + row(c, R_AF), -jnp.inf))
            d_b = jnp.exp2(jnp.where(upper, c_b + row(c, R_AB), -jnp.inf))
            qf32 = q.astype(F32)
            q_f = (qf32 * jnp.exp2(c_f[:, :2 * DQK])).astype(BF16)
            q_b = (qf32 * jnp.exp2(c_b[:, :2 * DQK] + chunk_scalar(c, R_TOTB))).astype(BF16)
            lhs = jnp.concatenate(
                [jnp.concatenate([(s * d_f).astype(BF16), q_f, zero_q], axis=1),
                 jnp.concatenate([(s * d_b).astype(BF16), zero_q, q_b], axis=1)], axis=0)
            rhs = jnp.concatenate([vaug(rows), pad_rows(sf_ref[c], zero_state),
                                   pad_rows(sb_ref[c], zero_state)], axis=0)
            nd = _dot(lhs, rhs)
            nd_f = nd[:CHUNK]
            nd_b = nd[CHUNK:]
            hh = (nd_f[:, :DV] / jnp.maximum(jnp.abs(nd_f[:, DV:]), 1.0)
                  + nd_b[:, :DV] / jnp.maximum(jnp.abs(nd_b[:, DV:]), 1.0))
            hn = hh * lax.rsqrt(jnp.mean(hh * hh, axis=-1, keepdims=True) + EPS)
            gated = hn * og_ref[rows, v_lanes].astype(F32)
            out_ref[rows, v_lanes] = gated.astype(BF16)
            return carry

        lax.fori_loop(0, nc, out_step, 0, unroll=4)


def _mlstm_call(q, kt, v, og, tab, rowtab, batch, seq):
    hd = HEADS * DV
    nqk = HEADS * DQK
    nc = seq // CHUNK
    seq_blk = lambda n: pl.BlockSpec((seq, n), lambda b: (b, 0))
    return pl.pallas_call(
        _mlstm_kernel,
        grid=(batch,),
        in_specs=[seq_blk(nqk), pl.BlockSpec((1, nc, nqk, CHUNK), lambda b: (b, 0, 0, 0)),
                  seq_blk(hd), seq_blk(hd), seq_blk(GATE_LANES),
                  pl.BlockSpec((1, nc, ROW_TABLE_ROWS, CHUNK), lambda b: (b, 0, 0, 0))],
        out_specs=seq_blk(hd),
        out_shape=jax.ShapeDtypeStruct((batch * seq, hd), BF16),
        scratch_shapes=[pltpu.VMEM((nc, 2 * DQK, 2 * DV), F32),
                        pltpu.VMEM((nc, DQK, 2 * DV), BF16), pltpu.VMEM((nc, DQK, 2 * DV), BF16)],
        compiler_params=_params(1),
        name="mlstm_core",
    )(q, kt, v, og, tab, rowtab)


def _mlp_kernel(*refs, has_scale, final_norm, ff_chunk):
    refs = list(refs)
    x_ref, a_ref, wo_ref = refs[:3]
    del refs[:3]
    sc_ref = refs.pop(0) if has_scale else None
    g_ref, w1_ref, w2_ref = refs[:3]
    del refs[:3]
    fg_ref = refs.pop(0) if final_norm else None
    (out_ref,) = refs

    r = _dot(a_ref[...], wo_ref[...])
    if has_scale:
        r = r * sc_ref[...]
    x1 = x_ref[...] + r
    u = _rms(x1, g_ref[...]).astype(BF16)
    y = x1
    for c in range(w1_ref.shape[1] // ff_chunk):
        cols = slice(c * ff_chunk, (c + 1) * ff_chunk)
        hid = jnp.square(jnp.maximum(_dot(u, w1_ref[:, cols]), 0.0)).astype(BF16)
        y = y + _dot(hid, w2_ref[cols, :])
    if final_norm:
        y = _rms(y, fg_ref[...])
    out_ref[...] = y


def _layer_spec(shape, layer):
    zeros = (0,) * (len(shape) - 1)
    return pl.BlockSpec((None,) + tuple(shape[1:]), lambda *_: (layer,) + zeros,
                        pipeline_mode=pl.Buffered(1))


def _mlp_call(x2, a, wo, scale, g, w1, w2, layer, final_g, tm, ff_chunk=1024):
    m, d = x2.shape
    row = lambda n: pl.BlockSpec((tm, n), lambda i: (i, 0))
    args = [x2, a, wo]
    specs = [row(d), row(a.shape[1]), _const_spec(wo.shape)]
    if scale is not None:
        args.append(scale)
        specs.append(_const_spec((1, d)))
    args += [g, w1, w2]
    specs += [_const_spec((1, d)), _layer_spec(w1.shape, layer), _layer_spec(w2.shape, layer)]
    if final_g is not None:
        args.append(final_g)
        specs.append(_const_spec((1, d)))
    body = functools.partial(_mlp_kernel, has_scale=scale is not None,
                             final_norm=final_g is not None, ff_chunk=ff_chunk)
    return pl.pallas_call(
        body,
        grid=(m // tm,),
        in_specs=specs,
        out_specs=row(d),
        out_shape=jax.ShapeDtypeStruct((m, d), F32),
        compiler_params=_params(1),
        name="mlp",
    )(*args)


def _pool_kernel(x_ref, g_ref, win_ref, wgrp_ref, out_ref, a_scr, *, proj_rows, pool_rows):
    seq, d = x_ref.shape
    gdim = d // len(POOL_WINDOWS)
    a_scr[0:POOL_HALO, :] = jnp.zeros((POOL_HALO, d), F32)
    a_scr[POOL_HALO + seq:, :] = jnp.zeros((POOL_HALO, d), F32)
    for t in range(seq // proj_rows):
        rows = slice(t * proj_rows, (t + 1) * proj_rows)
        u = _rms(x_ref[rows, :], g_ref[...]).astype(BF16)
        a_scr[POOL_HALO + t * proj_rows:POOL_HALO + (t + 1) * proj_rows, :] = _dot(u, win_ref[...])

    n_tiles = seq // pool_rows
    n = pool_rows + 2 * POOL_HALO
    centre = slice(POOL_HALO, POOL_HALO + pool_rows)
    for t in range(n_tiles):
        t0 = t * pool_rows
        for gi, w in enumerate(POOL_WINDOWS):
            cols = slice(gi * gdim, (gi + 1) * gdim)
            xw = a_scr[t0:t0 + n, cols]
            fwd = xw
            span = 1
            while 2 * span < w:
                fwd = fwd + pltpu.roll(fwd, n - span, axis=0)
                span *= 2
            s = fwd + pltpu.roll(fwd, w // 2, axis=0)
            if 0 < t < n_tiles - 1:
                mean = s[centre, :] * (1.0 / w)
            else:
                pos = t0 + lax.broadcasted_iota(jnp.int32, (pool_rows, gdim), 0)
                count = jnp.minimum(pos + w // 2, seq) - jnp.maximum(pos - w // 2, 0)
                mean = s[centre, :] / count.astype(F32)
            pooled = mean - xw[centre, :]
            mixed = _dot(pooled.astype(BF16), wgrp_ref[gi])
            out_ref[t0:t0 + pool_rows, cols] = mixed.astype(BF16)


def _pool_call(x2, g, w_in, w_group, batch, seq, proj_rows=512, pool_rows=256):
    d = x2.shape[1]
    seq_blk = pl.BlockSpec((seq, d), lambda b: (b, 0))
    body = functools.partial(_pool_kernel, proj_rows=proj_rows, pool_rows=pool_rows)
    return pl.pallas_call(
        body,
        grid=(batch,),
        in_specs=[seq_blk, _const_spec((1, d)), _const_spec(w_in.shape), _const_spec(w_group.shape)],
        out_specs=seq_blk,
        out_shape=jax.ShapeDtypeStruct((batch * seq, d), BF16),
        scratch_shapes=[pltpu.VMEM((seq + 2 * POOL_HALO, d), F32)],
        compiler_params=_params(1),
        name="pool_mixer",
    )(x2, g, w_in, w_group)


def kernel(x, mix_norm_g, mlp_norm_g, mlstm_w_in, mlstm_gate_b, mlstm_head_g, mlstm_w_out,
           pool_w_in, pool_w_group, pool_w_out, pool_scale, mlp_w1, mlp_w2, final_norm_g):
    batch, seq, d = x.shape
    assert seq % CHUNK == 0 and d == HEADS * DV
    m = batch * seq
    tm = 512
    x2 = x.reshape(m, d)
    vec = lambda a: a.reshape(1, -1).astype(F32)

    nqk = HEADS * DQK
    w_main = mlstm_w_in[0].astype(BF16)
    wkg = jnp.concatenate([w_main[:, nqk:2 * nqk], w_main[:, 2 * nqk + 2 * d:]], axis=1)
    gb = mlstm_gate_b[0].astype(F32).reshape(N_GATES, 1)
    w1 = mlp_w1.astype(BF16)
    w2 = mlp_w2.astype(BF16)

    q, kt, v, og, gates_t = _proj_call(x2, vec(mix_norm_g[0]), w_main, wkg, gb,
                                       vec(mlstm_head_g[0]), 1024, batch, seq)
    tab, rowtab = _gate_call(gates_t, batch, seq)
    hmix = _mlstm_call(q, kt, v, og, tab, rowtab, batch, seq)
    x2 = _mlp_call(x2, hmix, mlstm_w_out[0].astype(BF16), None, vec(mlp_norm_g[0]),
                   w1, w2, 0, None, tm)

    mixed = _pool_call(x2, vec(mix_norm_g[1]), pool_w_in[0].astype(BF16),
                       pool_w_group[0].astype(BF16), batch, seq)
    x2 = _mlp_call(x2, mixed, pool_w_out[0].astype(BF16), vec(pool_scale[0]), vec(mlp_norm_g[1]),
                   w1, w2, 1, vec(final_norm_g), tm)
    return x2.reshape(batch, seq, d)
```

```python
import functools

import numpy as np
import jax
import jax.numpy as jnp
from jax import lax
from jax.experimental import pallas as pl
from jax.experimental.pallas import tpu as pltpu

F32 = jnp.float32
BF16 = jnp.bfloat16

HEADS = 8
DQK = 64
DV = 128
CHUNK = 256
MLSTM_UNROLL = 8
GATE_SOFTCAP = 15.0
POOL_WINDOWS = (2, 4, 8, 16)
POOL_HALO = 8
EPS = 1e-6
LOG2_E = 1.4426950408889634
GATE_LANES = 128
VMEM_LIMIT = 56 * 1024 * 1024

N_GATES = 4 * HEADS
G_CF, G_CB = range(2)
R_AF, R_AB, R_TOTB, R_WKF, R_WKB, R_DECF, R_DECB = range(7)
ROW_TABLE_ROWS = 7 * HEADS


def _rms(x, g):
    ms = jnp.mean(x * x, axis=-1, keepdims=True)
    return x * lax.rsqrt(ms + EPS) * g


def _dot(a, b):
    return jnp.dot(a, b, preferred_element_type=F32)


def _dot_tn(a, b):
    return lax.dot_general(a, b, (((0,), (0,)), ((), ())), preferred_element_type=F32)


def _dot_nt(a, b):
    return lax.dot_general(a, b, (((1,), (1,)), ((), ())), preferred_element_type=F32)


def _split3(x):
    hi = x.astype(BF16)
    r1 = x - hi.astype(F32)
    mid = r1.astype(BF16)
    lo = (r1 - mid.astype(F32)).astype(BF16)
    return hi, mid, lo


def _dot3(parts, m):
    return _dot(parts[0], m) + _dot(parts[1], m) + _dot(parts[2], m)


def _dot3_left(m, parts):
    return _dot(m, parts[0]) + _dot(m, parts[1]) + _dot(m, parts[2])


def _const_spec(shape):
    zeros = (0,) * len(shape)
    return pl.BlockSpec(shape, lambda *_: zeros, pipeline_mode=pl.Buffered(1))


def _params(n_axes):
    return pltpu.CompilerParams(dimension_semantics=("arbitrary",) * n_axes,
                                vmem_limit_bytes=VMEM_LIMIT)


def _proj_kernel(x_ref, g_ref, w_ref, wkg_ref, gb_ref, hg_ref,
                 q_ref, kt_ref, v_ref, og_ref, gate_ref):
    nqk = HEADS * DQK
    hd = HEADS * DV
    u = _rms(x_ref[...], g_ref[...]).astype(BF16)
    q_ref[...] = (_dot(u, w_ref[:, 0:nqk]) * (DQK ** -0.5)).astype(BF16)
    ktg = lax.dot_general(wkg_ref[...], u, (((0,), (1,)), ((), ())), preferred_element_type=F32)
    kt = ktg[0:nqk].astype(BF16)
    for c in range(kt_ref.shape[1]):
        kt_ref[0, c] = kt[:, c * CHUNK:(c + 1) * CHUNK]
    g = ktg[nqk:nqk + N_GATES] + gb_ref[...]
    gate_ref[...] = GATE_SOFTCAP * jnp.tanh(g / GATE_SOFTCAP)
    v_ref[...] = _dot(u, w_ref[:, 2 * nqk:2 * nqk + hd]).astype(BF16)
    o = _dot(u, w_ref[:, 2 * nqk + hd:2 * nqk + 2 * hd])
    og_ref[...] = (jax.nn.sigmoid(o) * hg_ref[...]).astype(BF16)


def _proj_call(x2, g, w, wkg, gb, head_g, tm, batch, seq):
    m, d = x2.shape
    hd = HEADS * DV
    nqk = HEADS * DQK
    assert tm % CHUNK == 0 and seq % tm == 0
    tiles_per_seq = seq // tm
    row = lambda n: pl.BlockSpec((tm, n), lambda i: (i, 0))
    kt_spec = pl.BlockSpec((1, tm // CHUNK, nqk, CHUNK),
                           lambda i: (i // tiles_per_seq, i % tiles_per_seq, 0, 0))
    return pl.pallas_call(
        _proj_kernel,
        grid=(m // tm,),
        in_specs=[row(d), _const_spec((1, d)), _const_spec(w.shape), _const_spec(wkg.shape),
                  _const_spec((N_GATES, 1)), _const_spec((1, hd))],
        out_specs=[row(nqk), kt_spec, row(hd), row(hd),
                   pl.BlockSpec((N_GATES, tm), lambda i: (0, i))],
        out_shape=[jax.ShapeDtypeStruct((m, nqk), BF16),
                   jax.ShapeDtypeStruct((batch, seq // CHUNK, nqk, CHUNK), BF16),
                   jax.ShapeDtypeStruct((m, hd), BF16),
                   jax.ShapeDtypeStruct((m, hd), BF16),
                   jax.ShapeDtypeStruct((N_GATES, m), F32)],
        compiler_params=_params(1),
        name="mlstm_proj",
    )(x2, g, w, wkg, gb, head_g)


def _log_sigmoid(z):
    return jnp.minimum(z, 0.0) - jnp.log1p(jnp.exp(-jnp.abs(z)))


def _gate_kernel(gate_ref, triu_ref, ones_ref, tab_ref, row_ref):
    nc = row_ref.shape[1]
    g = gate_ref[...]
    i_f = g[0:HEADS]
    lf_f = _log_sigmoid(g[HEADS:2 * HEADS])
    i_b = g[2 * HEADS:3 * HEADS]
    lf_b = _log_sigmoid(g[3 * HEADS:4 * HEADS])

    def chunk(a, c):
        return a[:, c * CHUNK:(c + 1) * CHUNK]

    stacked = jnp.concatenate([chunk(lf_f, c) for c in range(nc)]
                              + [chunk(lf_b, c) for c in range(nc)], axis=0)
    parts = _split3(stacked)
    cum = _dot3(parts, triu_ref[...])
    tot = _dot3(parts, ones_ref[...])
    pad = jnp.zeros((GATE_LANES - 2 * HEADS, CHUNK), F32)
    for c in range(nc):
        fwd = slice(c * HEADS, (c + 1) * HEADS)
        bwd = slice((nc + c) * HEADS, (nc + c + 1) * HEADS)
        b_f, tot_f = cum[fwd], tot[fwd]
        e_b, tot_b = cum[bwd] - chunk(lf_b, c), tot[bwd]
        a_f = chunk(i_f, c) - b_f
        a_b = chunk(i_b, c) + e_b
        groups = {R_AF: a_f * LOG2_E, R_AB: a_b * LOG2_E, R_TOTB: tot_b * LOG2_E,
                  R_WKF: jnp.exp(tot_f + a_f), R_WKB: jnp.exp(a_b),
                  R_DECF: jnp.exp(tot_f), R_DECB: jnp.exp(tot_b)}
        for r, val in groups.items():
            row_ref[0, c, r * HEADS:(r + 1) * HEADS, :] = val
        cols = jnp.concatenate([b_f * LOG2_E, -e_b * LOG2_E, pad], axis=0)
        tab_ref[c * CHUNK:(c + 1) * CHUNK, :] = cols.T


def _gate_call(gates_t, batch, seq):
    triu = jnp.asarray(np.triu(np.ones((CHUNK, CHUNK), np.float32)), BF16)
    ones = jnp.ones((CHUNK, CHUNK), BF16)
    return pl.pallas_call(
        _gate_kernel,
        grid=(batch,),
        in_specs=[pl.BlockSpec((N_GATES, seq), lambda b: (0, b)),
                  _const_spec((CHUNK, CHUNK)), _const_spec((CHUNK, CHUNK))],
        out_specs=[pl.BlockSpec((seq, GATE_LANES), lambda b: (b, 0)),
                   pl.BlockSpec((1, seq // CHUNK, ROW_TABLE_ROWS, CHUNK), lambda b: (b, 0, 0, 0))],
        out_shape=[jax.ShapeDtypeStruct((batch * seq, GATE_LANES), F32),
                   jax.ShapeDtypeStruct((batch, seq // CHUNK, ROW_TABLE_ROWS, CHUNK), F32)],
        compiler_params=_params(1),
        name="mlstm_gates",
    )(gates_t, triu, ones)


def _mlstm_kernel(q_ref, kt_ref, v_ref, og_ref, tab_ref, row_ref, out_ref,
                  u_ref, sf_ref, sb_ref):
    nc = kt_ref.shape[1]
    ones_blk = jnp.ones((CHUNK, DV), BF16)
    zero_k = jnp.zeros((DQK, CHUNK), BF16)
    zero_state = jnp.zeros((DQK, 2 * DV), BF16)
    zero_q = jnp.zeros((CHUNK, 2 * DQK), BF16)
    t_idx = lax.broadcasted_iota(jnp.int32, (CHUNK, CHUNK), 0)
    s_idx = lax.broadcasted_iota(jnp.int32, (CHUNK, CHUNK), 1)
    lower = s_idx <= t_idx
    upper = s_idx >= t_idx

    def head_steps(h):
        pair_lanes = slice((h // 2) * 2 * DQK, (h // 2 + 1) * 2 * DQK)
        k_rows = slice(h * DQK, (h + 1) * DQK)
        v_lanes = slice(h * DV, (h + 1) * DV)

        def pad_rows(a, zero, h=h):
            return jnp.concatenate([a, zero] if h % 2 == 0 else [zero, a], axis=0)

        def chunk_rows(c):
            return pl.ds(pl.multiple_of(c * CHUNK, CHUNK), CHUNK)

        def vaug(rows, v_lanes=v_lanes):
            return jnp.concatenate([v_ref[rows, v_lanes], ones_blk], axis=1)

        def row(c, group, h=h):
            return row_ref[0, c, group * HEADS + h:group * HEADS + h + 1, :]

        def col(tab, group, h=h):
            return tab[:, group * HEADS + h:group * HEADS + h + 1]

        def contrib_step(c, carry):
            kt = kt_ref[0, c, k_rows, :].astype(F32)
            kw = jnp.concatenate([(kt * row(c, R_WKF)).astype(BF16),
                                  (kt * row(c, R_WKB)).astype(BF16)], axis=0)
            u_ref[c] = _dot(kw, vaug(chunk_rows(c)))
            return carry

        def chunk_scalar(c, group, h=h):
            return row_ref[0, c, group * HEADS + h:group * HEADS + h + 1, 0:1]

        def run_scans():
            state = jnp.zeros((DQK, 2 * DV), F32)
            for c in range(nc):
                sf_ref[c] = state.astype(BF16)
                state = chunk_scalar(c, R_DECF) * state + u_ref[c, 0:DQK, :]
            state = jnp.zeros((DQK, 2 * DV), F32)
            for c in reversed(range(nc)):
                sb_ref[c] = state.astype(BF16)
                state = chunk_scalar(c, R_DECB) * state + u_ref[c, DQK:2 * DQK, :]

        def out_step(c, carry):
            rows = chunk_rows(c)
            q = q_ref[rows, pair_lanes]
            s = _dot(q, pad_rows(kt_ref[0, c, k_rows, :], zero_k))
            tab = tab_ref[rows, :]
            c_f = jnp.broadcast_to(col(tab, G_CF), (CHUNK, CHUNK))
            c_b = jnp.broadcast_to(col(tab, G_CB), (CHUNK, CHUNK))
            d_f = jnp.exp2(jnp.where(lower, c_f + row(c, R_AF), -jnp.inf))
            d_b = jnp.exp2(jnp.where(upper, c_b + row(c, R_AB), -jnp.inf))
            qf32 = q.astype(F32)
            q_f = (qf32 * jnp.exp2(c_f[:, :2 * DQK])).astype(BF16)
            q_b = (qf32 * jnp.exp2(c_b[:, :2 * DQK] + chunk_scalar(c, R_TOTB))).astype(BF16)
            lhs = jnp.concatenate(
                [jnp.concatenate([(s * d_f).astype(BF16), q_f, zero_q], axis=1),
                 jnp.concatenate([(s * d_b).astype(BF16), zero_q, q_b], axis=1)], axis=0)
            rhs = jnp.concatenate([vaug(rows), pad_rows(sf_ref[c], zero_state),
                                   pad_rows(sb_ref[c], zero_state)], axis=0)
            nd = _dot(lhs, rhs)
            nd_f = nd[:CHUNK]
            nd_b = nd[CHUNK:]
            hh = (nd_f[:, :DV] / jnp.maximum(jnp.abs(nd_f[:, DV:]), 1.0)
                  + nd_b[:, :DV] / jnp.maximum(jnp.abs(nd_b[:, DV:]), 1.0))
            hn = hh * lax.rsqrt(jnp.mean(hh * hh, axis=-1, keepdims=True) + EPS)
            gated = hn * og_ref[rows, v_lanes].astype(F32)
            out_ref[rows, v_lanes] = gated.astype(BF16)
            return carry

        return contrib_step, run_scans, out_step

    steps = [head_steps(h) for h in range(HEADS)]
    lax.fori_loop(0, nc, steps[0][0], 0, unroll=MLSTM_UNROLL)
    steps[0][1]()
    for h in range(HEADS):
        out_step = steps[h][2]
        if h + 1 < HEADS:
            next_contrib = steps[h + 1][0]
            lax.fori_loop(0, nc, lambda c, carry: next_contrib(c, out_step(c, carry)), 0,
                          unroll=MLSTM_UNROLL)
            steps[h + 1][1]()
        else:
            lax.fori_loop(0, nc, out_step, 0, unroll=MLSTM_UNROLL)


def _mlstm_call(q, kt, v, og, tab, rowtab, batch, seq):
    hd = HEADS * DV
    nqk = HEADS * DQK
    nc = seq // CHUNK
    seq_blk = lambda n: pl.BlockSpec((seq, n), lambda b: (b, 0))
    return pl.pallas_call(
        _mlstm_kernel,
        grid=(batch,),
        in_specs=[seq_blk(nqk), pl.BlockSpec((1, nc, nqk, CHUNK), lambda b: (b, 0, 0, 0)),
                  seq_blk(hd), seq_blk(hd), seq_blk(GATE_LANES),
                  pl.BlockSpec((1, nc, ROW_TABLE_ROWS, CHUNK), lambda b: (b, 0, 0, 0))],
        out_specs=seq_blk(hd),
        out_shape=jax.ShapeDtypeStruct((batch * seq, hd), BF16),
        scratch_shapes=[pltpu.VMEM((nc, 2 * DQK, 2 * DV), F32),
                        pltpu.VMEM((nc, DQK, 2 * DV), BF16), pltpu.VMEM((nc, DQK, 2 * DV), BF16)],
        compiler_params=_params(1),
        name="mlstm_core",
    )(q, kt, v, og, tab, rowtab)


def _mlp_kernel(*refs, has_scale, final_norm, ff_chunk):
    refs = list(refs)
    x_ref, a_ref, wo_ref = refs[:3]
    del refs[:3]
    sc_ref = refs.pop(0) if has_scale else None
    g_ref, w1_ref, w2_ref = refs[:3]
    del refs[:3]
    fg_ref = refs.pop(0) if final_norm else None
    (out_ref,) = refs

    r = _dot(a_ref[...], wo_ref[...])
    if has_scale:
        r = r * sc_ref[...]
    x1 = x_ref[...] + r
    u = _rms(x1, g_ref[...]).astype(BF16)
    y = x1
    for c in range(w1_ref.shape[1] // ff_chunk):
        cols = slice(c * ff_chunk, (c + 1) * ff_chunk)
        hid = jnp.square(jnp.maximum(_dot(u, w1_ref[:, cols]), 0.0)).astype(BF16)
        y = y + _dot(hid, w2_ref[cols, :])
    if final_norm:
        y = _rms(y, fg_ref[...])
    out_ref[...] = y


def _layer_spec(shape, layer):
    zeros = (0,) * (len(shape) - 1)
    return pl.BlockSpec((None,) + tuple(shape[1:]), lambda *_: (layer,) + zeros,
                        pipeline_mode=pl.Buffered(1))


def _mlp_call(x2, a, wo, scale, g, w1, w2, layer, final_g, tm, ff_chunk=1024):
    m, d = x2.shape
    row = lambda n: pl.BlockSpec((tm, n), lambda i: (i, 0))
    args = [x2, a, wo]
    specs = [row(d), row(a.shape[1]), _const_spec(wo.shape)]
    if scale is not None:
        args.append(scale)
        specs.append(_const_spec((1, d)))
    args += [g, w1, w2]
    specs += [_const_spec((1, d)), _layer_spec(w1.shape, layer), _layer_spec(w2.shape, layer)]
    if final_g is not None:
        args.append(final_g)
        specs.append(_const_spec((1, d)))
    body = functools.partial(_mlp_kernel, has_scale=scale is not None,
                             final_norm=final_g is not None, ff_chunk=ff_chunk)
    return pl.pallas_call(
        body,
        grid=(m // tm,),
        in_specs=specs,
        out_specs=row(d),
        out_shape=jax.ShapeDtypeStruct((m, d), F32),
        compiler_params=_params(1),
        name="mlp",
    )(*args)


def _pool_kernel(x_ref, g_ref, win_ref, wgrp_ref, out_ref, a_scr, *, proj_rows, pool_rows):
    seq, d = x_ref.shape
    gdim = d // len(POOL_WINDOWS)
    a_scr[0:POOL_HALO, :] = jnp.zeros((POOL_HALO, d), F32)
    a_scr[POOL_HALO + seq:, :] = jnp.zeros((POOL_HALO, d), F32)
    for t in range(seq // proj_rows):
        rows = slice(t * proj_rows, (t + 1) * proj_rows)
        u = _rms(x_ref[rows, :], g_ref[...]).astype(BF16)
        a_scr[POOL_HALO + t * proj_rows:POOL_HALO + (t + 1) * proj_rows, :] = _dot(u, win_ref[...])

    n_tiles = seq // pool_rows
    n = pool_rows + 2 * POOL_HALO
    centre = slice(POOL_HALO, POOL_HALO + pool_rows)
    for t in range(n_tiles):
        t0 = t * pool_rows
        for gi, w in enumerate(POOL_WINDOWS):
            cols = slice(gi * gdim, (gi + 1) * gdim)
            xw = a_scr[t0:t0 + n, cols]
            fwd = xw
            span = 1
            while 2 * span < w:
                fwd = fwd + pltpu.roll(fwd, n - span, axis=0)
                span *= 2
            s = fwd + pltpu.roll(fwd, w // 2, axis=0)
            if 0 < t < n_tiles - 1:
                mean = s[centre, :] * (1.0 / w)
            else:
                pos = t0 + lax.broadcasted_iota(jnp.int32, (pool_rows, gdim), 0)
                count = jnp.minimum(pos + w // 2, seq) - jnp.maximum(pos - w // 2, 0)
                mean = s[centre, :] / count.astype(F32)
            pooled = mean - xw[centre, :]
            mixed = _dot(pooled.astype(BF16), wgrp_ref[gi])
            out_ref[t0:t0 + pool_rows, cols] = mixed.astype(BF16)


def _pool_call(x2, g, w_in, w_group, batch, seq, proj_rows=512, pool_rows=256):
    d = x2.shape[1]
    seq_blk = pl.BlockSpec((seq, d), lambda b: (b, 0))
    body = functools.partial(_pool_kernel, proj_rows=proj_rows, pool_rows=pool_rows)
    return pl.pallas_call(
        body,
        grid=(batch,),
        in_specs=[seq_blk, _const_spec((1, d)), _const_spec(w_in.shape), _const_spec(w_group.shape)],
        out_specs=seq_blk,
        out_shape=jax.ShapeDtypeStruct((batch * seq, d), BF16),
        scratch_shapes=[pltpu.VMEM((seq + 2 * POOL_HALO, d), F32)],
        compiler_params=_params(1),
        name="pool_mixer",
    )(x2, g, w_in, w_group)


def kernel(x, mix_norm_g, mlp_norm_g, mlstm_w_in, mlstm_gate_b, mlstm_head_g, mlstm_w_out,
           pool_w_in, pool_w_group, pool_w_out, pool_scale, mlp_w1, mlp_w2, final_norm_g):
    batch, seq, d = x.shape
    assert seq % CHUNK == 0 and d == HEADS * DV
    m = batch * seq
    tm = 512
    x2 = x.reshape(m, d)
    vec = lambda a: a.reshape(1, -1).astype(F32)

    nqk = HEADS * DQK
    w_main = mlstm_w_in[0].astype(BF16)
    wkg = jnp.concatenate([w_main[:, nqk:2 * nqk], w_main[:, 2 * nqk + 2 * d:]], axis=1)
    gb = mlstm_gate_b[0].astype(F32).reshape(N_GATES, 1)
    w1 = mlp_w1.astype(BF16)
    w2 = mlp_w2.astype(BF16)

    q, kt, v, og, gates_t = _proj_call(x2, vec(mix_norm_g[0]), w_main, wkg, gb,
                                       vec(mlstm_head_g[0]), 1024, batch, seq)
    tab, rowtab = _gate_call(gates_t, batch, seq)
    hmix = _mlstm_call(q, kt, v, og, tab, rowtab, batch, seq)
    x2 = _mlp_call(x2, hmix, mlstm_w_out[0].astype(BF16), None, vec(mlp_norm_g[0]),
                   w1, w2, 0, None, tm)

    mixed = _pool_call(x2, vec(mix_norm_g[1]), pool_w_in[0].astype(BF16),
                       pool_w_group[0].astype(BF16), batch, seq)
    x2 = _mlp_call(x2, mixed, pool_w_out[0].astype(BF16), vec(pool_scale[0]), vec(mlp_norm_g[1]),
                   w1, w2, 1, vec(final_norm_g), tm)
    return x2.reshape(batch, seq, d)
```

```python
import functools

import numpy as np
import jax
import jax.numpy as jnp
from jax import lax
from jax.experimental import pallas as pl
from jax.experimental.pallas import tpu as pltpu

F32 = jnp.float32
BF16 = jnp.bfloat16

HEADS = 8
DQK = 64
DV = 128
CHUNK = 256
MLSTM_UNROLL = 8
GATE_SOFTCAP = 15.0
POOL_WINDOWS = (2, 4, 8, 16)
POOL_HALO = 8
EPS = 1e-6
LOG2_E = 1.4426950408889634
GATE_LANES = 128
VMEM_LIMIT = 56 * 1024 * 1024

N_GATES = 4 * HEADS
G_CF, G_CB = range(2)
R_AF, R_AB, R_TOTB, R_WKF, R_WKB, R_DECF, R_DECB = range(7)
ROW_TABLE_ROWS = 7 * HEADS


def _rms(x, g):
    ms = jnp.mean(x * x, axis=-1, keepdims=True)
    return x * lax.rsqrt(ms + EPS) * g


def _dot(a, b):
    return jnp.dot(a, b, preferred_element_type=F32)


def _dot_tn(a, b):
    return lax.dot_general(a, b, (((0,), (0,)), ((), ())), preferred_element_type=F32)


def _dot_nt(a, b):
    return lax.dot_general(a, b, (((1,), (1,)), ((), ())), preferred_element_type=F32)


def _split3(x):
    hi = x.astype(BF16)
    r1 = x - hi.astype(F32)
    mid = r1.astype(BF16)
    lo = (r1 - mid.astype(F32)).astype(BF16)
    return hi, mid, lo


def _dot3(parts, m):
    return _dot(parts[0], m) + _dot(parts[1], m) + _dot(parts[2], m)


def _dot3_left(m, parts):
    return _dot(m, parts[0]) + _dot(m, parts[1]) + _dot(m, parts[2])


def _const_spec(shape):
    zeros = (0,) * len(shape)
    return pl.BlockSpec(shape, lambda *_: zeros, pipeline_mode=pl.Buffered(1))


def _params(n_axes):
    return pltpu.CompilerParams(dimension_semantics=("arbitrary",) * n_axes,
                                vmem_limit_bytes=VMEM_LIMIT)


def _proj_kernel(*refs, n_cast):
    x_ref, g_ref, w_ref, wkg_ref, gb_ref, hg_ref = refs[:6]
    cast_in = refs[6:6 + n_cast]
    q_ref, kt_ref, v_ref, og_ref, gate_ref = refs[6 + n_cast:11 + n_cast]
    cast_out = refs[11 + n_cast:]
    for src, dst in zip(cast_in, cast_out):
        dst[...] = src[...].astype(BF16)
    nqk = HEADS * DQK
    hd = HEADS * DV
    u = _rms(x_ref[...], g_ref[...]).astype(BF16)
    q_ref[...] = (_dot(u, w_ref[:, 0:nqk]) * (DQK ** -0.5)).astype(BF16)
    ktg = lax.dot_general(wkg_ref[...], u, (((0,), (1,)), ((), ())), preferred_element_type=F32)
    kt = ktg[0:nqk].astype(BF16)
    for c in range(kt_ref.shape[1]):
        kt_ref[0, c] = kt[:, c * CHUNK:(c + 1) * CHUNK]
    g = ktg[nqk:nqk + N_GATES] + gb_ref[...]
    gate_ref[...] = GATE_SOFTCAP * jnp.tanh(g / GATE_SOFTCAP)
    v_ref[...] = _dot(u, w_ref[:, 2 * nqk:2 * nqk + hd]).astype(BF16)
    o = _dot(u, w_ref[:, 2 * nqk + hd:2 * nqk + 2 * hd])
    og_ref[...] = (jax.nn.sigmoid(o) * hg_ref[...]).astype(BF16)


def _row_slab_spec(shape, steps):
    rows = shape[-2]
    assert rows % steps == 0 and (rows // steps) % 16 == 0
    block = tuple(shape[:-2]) + (rows // steps, shape[-1])
    lead = (0,) * (len(shape) - 2)
    return pl.BlockSpec(block, lambda i: lead + (i, 0))


def _proj_call(x2, g, w, wkg, gb, head_g, to_cast, tm, batch, seq):
    m, d = x2.shape
    hd = HEADS * DV
    nqk = HEADS * DQK
    assert tm % CHUNK == 0 and seq % tm == 0
    tiles_per_seq = seq // tm
    steps = m // tm
    row = lambda n: pl.BlockSpec((tm, n), lambda i: (i, 0))
    kt_spec = pl.BlockSpec((1, tm // CHUNK, nqk, CHUNK),
                           lambda i: (i // tiles_per_seq, i % tiles_per_seq, 0, 0))
    cast_specs = [_row_slab_spec(a.shape, steps) for a in to_cast]
    outs = pl.pallas_call(
        functools.partial(_proj_kernel, n_cast=len(to_cast)),
        grid=(steps,),
        in_specs=[row(d), _const_spec((1, d)), _const_spec(w.shape), _const_spec(wkg.shape),
                  _const_spec((N_GATES, 1)), _const_spec((1, hd))] + cast_specs,
        out_specs=[row(nqk), kt_spec, row(hd), row(hd),
                   pl.BlockSpec((N_GATES, tm), lambda i: (0, i))] + cast_specs,
        out_shape=[jax.ShapeDtypeStruct((m, nqk), BF16),
                   jax.ShapeDtypeStruct((batch, seq // CHUNK, nqk, CHUNK), BF16),
                   jax.ShapeDtypeStruct((m, hd), BF16),
                   jax.ShapeDtypeStruct((m, hd), BF16),
                   jax.ShapeDtypeStruct((N_GATES, m), F32)]
                  + [jax.ShapeDtypeStruct(a.shape, BF16) for a in to_cast],
        compiler_params=_params(1),
        name="mlstm_proj",
    )(x2, g, w, wkg, gb, head_g, *to_cast)
    return outs[:5], outs[5:]


def _log_sigmoid(z):
    return jnp.minimum(z, 0.0) - jnp.log1p(jnp.exp(-jnp.abs(z)))


def _gate_kernel(gate_ref, triu_ref, ones_ref, tab_ref, row_ref):
    nc = row_ref.shape[1]
    g = gate_ref[...]
    i_f = g[0:HEADS]
    lf_f = _log_sigmoid(g[HEADS:2 * HEADS])
    i_b = g[2 * HEADS:3 * HEADS]
    lf_b = _log_sigmoid(g[3 * HEADS:4 * HEADS])

    def chunk(a, c):
        return a[:, c * CHUNK:(c + 1) * CHUNK]

    stacked = jnp.concatenate([chunk(lf_f, c) for c in range(nc)]
                              + [chunk(lf_b, c) for c in range(nc)], axis=0)
    parts = _split3(stacked)
    cum = _dot3(parts, triu_ref[...])
    tot = _dot3(parts, ones_ref[...])
    pad = jnp.zeros((GATE_LANES - 2 * HEADS, CHUNK), F32)
    for c in range(nc):
        fwd = slice(c * HEADS, (c + 1) * HEADS)
        bwd = slice((nc + c) * HEADS, (nc + c + 1) * HEADS)
        b_f, tot_f = cum[fwd], tot[fwd]
        e_b, tot_b = cum[bwd] - chunk(lf_b, c), tot[bwd]
        a_f = chunk(i_f, c) - b_f
        a_b = chunk(i_b, c) + e_b
        groups = {R_AF: a_f * LOG2_E, R_AB: a_b * LOG2_E, R_TOTB: tot_b * LOG2_E,
                  R_WKF: jnp.exp(tot_f + a_f), R_WKB: jnp.exp(a_b),
                  R_DECF: jnp.exp(tot_f), R_DECB: jnp.exp(tot_b)}
        for r, val in groups.items():
            row_ref[0, c, r * HEADS:(r + 1) * HEADS, :] = val
        cols = jnp.concatenate([b_f * LOG2_E, -e_b * LOG2_E, pad], axis=0)
        tab_ref[c * CHUNK:(c + 1) * CHUNK, :] = cols.T


def _gate_call(gates_t, batch, seq):
    triu = jnp.asarray(np.triu(np.ones((CHUNK, CHUNK), np.float32)), BF16)
    ones = jnp.ones((CHUNK, CHUNK), BF16)
    return pl.pallas_call(
        _gate_kernel,
        grid=(batch,),
        in_specs=[pl.BlockSpec((N_GATES, seq), lambda b: (0, b)),
                  _const_spec((CHUNK, CHUNK)), _const_spec((CHUNK, CHUNK))],
        out_specs=[pl.BlockSpec((seq, GATE_LANES), lambda b: (b, 0)),
                   pl.BlockSpec((1, seq // CHUNK, ROW_TABLE_ROWS, CHUNK), lambda b: (b, 0, 0, 0))],
        out_shape=[jax.ShapeDtypeStruct((batch * seq, GATE_LANES), F32),
                   jax.ShapeDtypeStruct((batch, seq // CHUNK, ROW_TABLE_ROWS, CHUNK), F32)],
        compiler_params=_params(1),
        name="mlstm_gates",
    )(gates_t, triu, ones)


def _mlstm_kernel(q_ref, kt_ref, v_ref, og_ref, tab_ref, row_ref, out_ref,
                  u_ref, sf_ref, sb_ref):
    nc = kt_ref.shape[1]
    ones_blk = jnp.ones((CHUNK, DV), BF16)
    zero_k = jnp.zeros((DQK, CHUNK), BF16)
    zero_state = jnp.zeros((DQK, 2 * DV), BF16)
    zero_q = jnp.zeros((CHUNK, 2 * DQK), BF16)
    t_idx = lax.broadcasted_iota(jnp.int32, (CHUNK, CHUNK), 0)
    s_idx = lax.broadcasted_iota(jnp.int32, (CHUNK, CHUNK), 1)
    lower = s_idx <= t_idx
    upper = s_idx >= t_idx

    def head_steps(h):
        pair_lanes = slice((h // 2) * 2 * DQK, (h // 2 + 1) * 2 * DQK)
        k_rows = slice(h * DQK, (h + 1) * DQK)
        v_lanes = slice(h * DV, (h + 1) * DV)

        def pad_rows(a, zero, h=h):
            return jnp.concatenate([a, zero] if h % 2 == 0 else [zero, a], axis=0)

        def chunk_rows(c):
            return pl.ds(pl.multiple_of(c * CHUNK, CHUNK), CHUNK)

        def vaug(rows, v_lanes=v_lanes):
            return jnp.concatenate([v_ref[rows, v_lanes], ones_blk], axis=1)

        def row(c, group, h=h):
            return row_ref[0, c, group * HEADS + h:group * HEADS + h + 1, :]

        def col(tab, group, h=h):
            return tab[:, group * HEADS + h:group * HEADS + h + 1]

        def contrib_step(c, carry):
            kt = kt_ref[0, c, k_rows, :].astype(F32)
            kw = jnp.concatenate([(kt * row(c, R_WKF)).astype(BF16),
                                  (kt * row(c, R_WKB)).astype(BF16)], axis=0)
            u_ref[c] = _dot(kw, vaug(chunk_rows(c)))
            return carry

        def chunk_scalar(c, group, h=h):
            return row_ref[0, c, group * HEADS + h:group * HEADS + h + 1, 0:1]

        def run_scans():
            state = jnp.zeros((DQK, 2 * DV), F32)
            for c in range(nc):
                sf_ref[c] = state.astype(BF16)
                state = chunk_scalar(c, R_DECF) * state + u_ref[c, 0:DQK, :]
            state = jnp.zeros((DQK, 2 * DV), F32)
            for c in reversed(range(nc)):
                sb_ref[c] = state.astype(BF16)
                state = chunk_scalar(c, R_DECB) * state + u_ref[c, DQK:2 * DQK, :]

        def out_step(c, carry):
            rows = chunk_rows(c)
            q = q_ref[rows, pair_lanes]
            s = _dot(q, pad_rows(kt_ref[0, c, k_rows, :], zero_k))
            tab = tab_ref[rows, :]
            c_f = jnp.broadcast_to(col(tab, G_CF), (CHUNK, CHUNK))
            c_b = jnp.broadcast_to(col(tab, G_CB), (CHUNK, CHUNK))
            d_f = jnp.exp2(jnp.where(lower, c_f + row(c, R_AF), -jnp.inf))
            d_b = jnp.exp2(jnp.where(upper, c_b + row(c, R_AB), -jnp.inf))
            qf32 = q.astype(F32)
            q_f = (qf32 * jnp.exp2(c_f[:, :2 * DQK])).astype(BF16)
            q_b = (qf32 * jnp.exp2(c_b[:, :2 * DQK] + chunk_scalar(c, R_TOTB))).astype(BF16)
            lhs = jnp.concatenate(
                [jnp.concatenate([(s * d_f).astype(BF16), q_f, zero_q], axis=1),
                 jnp.concatenate([(s * d_b).astype(BF16), zero_q, q_b], axis=1)], axis=0)
            rhs = jnp.concatenate([vaug(rows), pad_rows(sf_ref[c], zero_state),
                                   pad_rows(sb_ref[c], zero_state)], axis=0)
            nd = _dot(lhs, rhs)
            nd_f = nd[:CHUNK]
            nd_b = nd[CHUNK:]
            hh = (nd_f[:, :DV] / jnp.maximum(jnp.abs(nd_f[:, DV:]), 1.0)
                  + nd_b[:, :DV] / jnp.maximum(jnp.abs(nd_b[:, DV:]), 1.0))
            hn = hh * lax.rsqrt(jnp.mean(hh * hh, axis=-1, keepdims=True) + EPS)
            gated = hn * og_ref[rows, v_lanes].astype(F32)
            out_ref[rows, v_lanes] = gated.astype(BF16)
            return carry

        return contrib_step, run_scans, out_step

    steps = [head_steps(h) for h in range(HEADS)]
    lax.fori_loop(0, nc, steps[0][0], 0, unroll=MLSTM_UNROLL)
    steps[0][1]()
    for h in range(HEADS):
        out_step = steps[h][2]
        if h + 1 < HEADS:
            next_contrib = steps[h + 1][0]
            lax.fori_loop(0, nc, lambda c, carry: next_contrib(c, out_step(c, carry)), 0,
                          unroll=MLSTM_UNROLL)
            steps[h + 1][1]()
        else:
            lax.fori_loop(0, nc, out_step, 0, unroll=MLSTM_UNROLL)


def _mlstm_call(q, kt, v, og, tab, rowtab, batch, seq):
    hd = HEADS * DV
    nqk = HEADS * DQK
    nc = seq // CHUNK
    seq_blk = lambda n: pl.BlockSpec((seq, n), lambda b: (b, 0))
    return pl.pallas_call(
        _mlstm_kernel,
        grid=(batch,),
        in_specs=[seq_blk(nqk), pl.BlockSpec((1, nc, nqk, CHUNK), lambda b: (b, 0, 0, 0)),
                  seq_blk(hd), seq_blk(hd), seq_blk(GATE_LANES),
                  pl.BlockSpec((1, nc, ROW_TABLE_ROWS, CHUNK), lambda b: (b, 0, 0, 0))],
        out_specs=seq_blk(hd),
        out_shape=jax.ShapeDtypeStruct((batch * seq, hd), BF16),
        scratch_shapes=[pltpu.VMEM((nc, 2 * DQK, 2 * DV), F32),
                        pltpu.VMEM((nc, DQK, 2 * DV), BF16), pltpu.VMEM((nc, DQK, 2 * DV), BF16)],
        compiler_params=_params(1),
        name="mlstm_core",
    )(q, kt, v, og, tab, rowtab)


def _mlp_kernel(*refs, has_scale, final_norm, ff_chunk):
    refs = list(refs)
    x_ref, a_ref, wo_ref = refs[:3]
    del refs[:3]
    sc_ref = refs.pop(0) if has_scale else None
    g_ref, w1_ref, w2_ref = refs[:3]
    del refs[:3]
    fg_ref = refs.pop(0) if final_norm else None
    (out_ref,) = refs

    r = _dot(a_ref[...], wo_ref[...])
    if has_scale:
        r = r * sc_ref[...]
    x1 = x_ref[...] + r
    u = _rms(x1, g_ref[...]).astype(BF16)
    y = x1
    for c in range(w1_ref.shape[1] // ff_chunk):
        cols = slice(c * ff_chunk, (c + 1) * ff_chunk)
        hid = jnp.square(jnp.maximum(_dot(u, w1_ref[:, cols]), 0.0)).astype(BF16)
        y = y + _dot(hid, w2_ref[cols, :])
    if final_norm:
        y = _rms(y, fg_ref[...])
    out_ref[...] = y


def _layer_spec(shape, layer):
    zeros = (0,) * (len(shape) - 1)
    return pl.BlockSpec((None,) + tuple(shape[1:]), lambda *_: (layer,) + zeros,
                        pipeline_mode=pl.Buffered(1))


def _mlp_call(x2, a, wo, scale, g, w1, w2, layer, final_g, tm, ff_chunk=1024):
    m, d = x2.shape
    row = lambda n: pl.BlockSpec((tm, n), lambda i: (i, 0))
    args = [x2, a, wo]
    specs = [row(d), row(a.shape[1]), _const_spec(wo.shape)]
    if scale is not None:
        args.append(scale)
        specs.append(_const_spec((1, d)))
    args += [g, w1, w2]
    specs += [_const_spec((1, d)), _layer_spec(w1.shape, layer), _layer_spec(w2.shape, layer)]
    if final_g is not None:
        args.append(final_g)
        specs.append(_const_spec((1, d)))
    body = functools.partial(_mlp_kernel, has_scale=scale is not None,
                             final_norm=final_g is not None, ff_chunk=ff_chunk)
    return pl.pallas_call(
        body,
        grid=(m // tm,),
        in_specs=specs,
        out_specs=row(d),
        out_shape=jax.ShapeDtypeStruct((m, d), F32),
        compiler_params=_params(1),
        name="mlp",
    )(*args)


def _pool_kernel(x_ref, g_ref, win_ref, wgrp_ref, out_ref, a_scr, *, proj_rows, pool_rows):
    seq, d = x_ref.shape
    gdim = d // len(POOL_WINDOWS)
    a_scr[0:POOL_HALO, :] = jnp.zeros((POOL_HALO, d), F32)
    a_scr[POOL_HALO + seq:, :] = jnp.zeros((POOL_HALO, d), F32)
    for t in range(seq // proj_rows):
        rows = slice(t * proj_rows, (t + 1) * proj_rows)
        u = _rms(x_ref[rows, :], g_ref[...]).astype(BF16)
        a_scr[POOL_HALO + t * proj_rows:POOL_HALO + (t + 1) * proj_rows, :] = _dot(u, win_ref[...])

    n_tiles = seq // pool_rows
    n = pool_rows + 2 * POOL_HALO
    centre = slice(POOL_HALO, POOL_HALO + pool_rows)
    for t in range(n_tiles):
        t0 = t * pool_rows
        for gi, w in enumerate(POOL_WINDOWS):
            cols = slice(gi * gdim, (gi + 1) * gdim)
            xw = a_scr[t0:t0 + n, cols]
            fwd = xw
            span = 1
            while 2 * span < w:
                fwd = fwd + pltpu.roll(fwd, n - span, axis=0)
                span *= 2
            s = fwd + pltpu.roll(fwd, w // 2, axis=0)
            if 0 < t < n_tiles - 1:
                mean = s[centre, :] * (1.0 / w)
            else:
                pos = t0 + lax.broadcasted_iota(jnp.int32, (pool_rows, gdim), 0)
                count = jnp.minimum(pos + w // 2, seq) - jnp.maximum(pos - w // 2, 0)
                mean = s[centre, :] / count.astype(F32)
            pooled = mean - xw[centre, :]
            mixed = _dot(pooled.astype(BF16), wgrp_ref[gi])
            out_ref[t0:t0 + pool_rows, cols] = mixed.astype(BF16)


def _pool_call(x2, g, w_in, w_group, batch, seq, proj_rows=512, pool_rows=256):
    d = x2.shape[1]
    seq_blk = pl.BlockSpec((seq, d), lambda b: (b, 0))
    body = functools.partial(_pool_kernel, proj_rows=proj_rows, pool_rows=pool_rows)
    return pl.pallas_call(
        body,
        grid=(batch,),
        in_specs=[seq_blk, _const_spec((1, d)), _const_spec(w_in.shape), _const_spec(w_group.shape)],
        out_specs=seq_blk,
        out_shape=jax.ShapeDtypeStruct((batch * seq, d), BF16),
        scratch_shapes=[pltpu.VMEM((seq + 2 * POOL_HALO, d), F32)],
        compiler_params=_params(1),
        name="pool_mixer",
    )(x2, g, w_in, w_group)


def kernel(x, mix_norm_g, mlp_norm_g, mlstm_w_in, mlstm_gate_b, mlstm_head_g, mlstm_w_out,
           pool_w_in, pool_w_group, pool_w_out, pool_scale, mlp_w1, mlp_w2, final_norm_g):
    batch, seq, d = x.shape
    assert seq % CHUNK == 0 and d == HEADS * DV
    m = batch * seq
    tm = 512
    x2 = x.reshape(m, d)
    vec = lambda a: a.reshape(1, -1).astype(F32)

    nqk = HEADS * DQK
    w_main = mlstm_w_in[0].astype(BF16)
    wkg = jnp.concatenate([w_main[:, nqk:2 * nqk], w_main[:, 2 * nqk + 2 * d:]], axis=1)
    gb = mlstm_gate_b[0].astype(F32).reshape(N_GATES, 1)
    n_groups, gdim = pool_w_group.shape[1], pool_w_group.shape[2]

    later_f32 = [mlp_w1, mlp_w2, mlstm_w_out[0], pool_w_in[0], pool_w_out[0],
                 pool_w_group[0].reshape(n_groups * gdim, gdim)]
    (q, kt, v, og, gates_t), (w1, w2, w_out, p_in, p_out, p_group) = _proj_call(
        x2, vec(mix_norm_g[0]), w_main, wkg, gb, vec(mlstm_head_g[0]), later_f32, 1024, batch, seq)
    tab, rowtab = _gate_call(gates_t, batch, seq)
    hmix = _mlstm_call(q, kt, v, og, tab, rowtab, batch, seq)
    x2 = _mlp_call(x2, hmix, w_out, None, vec(mlp_norm_g[0]), w1, w2, 0, None, tm)

    mixed = _pool_call(x2, vec(mix_norm_g[1]), p_in, p_group.reshape(n_groups, gdim, gdim),
                       batch, seq)
    x2 = _mlp_call(x2, mixed, p_out, vec(pool_scale[0]), vec(mlp_norm_g[1]),
                   w1, w2, 1, vec(final_norm_g), tm)
    return x2.reshape(batch, seq, d)
```

```python
import functools

import numpy as np
import jax
import jax.numpy as jnp
from jax import lax
from jax.experimental import pallas as pl
from jax.experimental.pallas import tpu as pltpu

F32 = jnp.float32
BF16 = jnp.bfloat16

HEADS = 8
DQK = 64
DV = 128
CHUNK = 256
MLSTM_UNROLL = 8
GATE_SOFTCAP = 15.0
POOL_WINDOWS = (2, 4, 8, 16)
POOL_HALO = 8
EPS = 1e-6
LOG2_E = 1.4426950408889634
GATE_LANES = 128
VMEM_LIMIT = 56 * 1024 * 1024

N_GATES = 4 * HEADS
G_CF, G_CB = range(2)
R_AF, R_AB, R_TOTB, R_WKF, R_WKB, R_DECF, R_DECB = range(7)
ROW_TABLE_ROWS = 7 * HEADS


def _rms(x, g):
    ms = jnp.mean(x * x, axis=-1, keepdims=True)
    return x * lax.rsqrt(ms + EPS) * g


def _dot(a, b):
    return jnp.dot(a, b, preferred_element_type=F32)


def _dot_tn(a, b):
    return lax.dot_general(a, b, (((0,), (0,)), ((), ())), preferred_element_type=F32)


def _dot_nt(a, b):
    return lax.dot_general(a, b, (((1,), (1,)), ((), ())), preferred_element_type=F32)


def _split3(x):
    hi = x.astype(BF16)
    r1 = x - hi.astype(F32)
    mid = r1.astype(BF16)
    lo = (r1 - mid.astype(F32)).astype(BF16)
    return hi, mid, lo


def _dot3(parts, m):
    return _dot(parts[0], m) + _dot(parts[1], m) + _dot(parts[2], m)


def _dot3_left(m, parts):
    return _dot(m, parts[0]) + _dot(m, parts[1]) + _dot(m, parts[2])


def _const_spec(shape):
    zeros = (0,) * len(shape)
    return pl.BlockSpec(shape, lambda *_: zeros, pipeline_mode=pl.Buffered(1))


def _params(n_axes):
    return pltpu.CompilerParams(dimension_semantics=("arbitrary",) * n_axes,
                                vmem_limit_bytes=VMEM_LIMIT)


def _proj_kernel(*refs, n_cast):
    x_ref, g_ref, w_ref, wkg_ref, gb_ref, hg_ref = refs[:6]
    cast_in = refs[6:6 + n_cast]
    q_ref, kt_ref, v_ref, og_ref, gate_ref = refs[6 + n_cast:11 + n_cast]
    cast_out = refs[11 + n_cast:]
    for src, dst in zip(cast_in, cast_out):
        dst[...] = src[...].astype(BF16)
    nqk = HEADS * DQK
    hd = HEADS * DV

    def w_cols(lo, hi):
        return w_ref[:, lo:hi].astype(BF16)

    u = _rms(x_ref[...], g_ref[...]).astype(BF16)
    q_ref[...] = (_dot(u, w_cols(0, nqk)) * (DQK ** -0.5)).astype(BF16)
    ktg = lax.dot_general(wkg_ref[...], u, (((0,), (1,)), ((), ())), preferred_element_type=F32)
    kt = ktg[0:nqk].astype(BF16)
    for c in range(kt_ref.shape[1]):
        kt_ref[0, c] = kt[:, c * CHUNK:(c + 1) * CHUNK]
    g = ktg[nqk:nqk + N_GATES] + gb_ref[...]
    gate_ref[...] = GATE_SOFTCAP * jnp.tanh(g / GATE_SOFTCAP)
    v_ref[...] = _dot(u, w_cols(2 * nqk, 2 * nqk + hd)).astype(BF16)
    o = _dot(u, w_cols(2 * nqk + hd, 2 * nqk + 2 * hd))
    og_ref[...] = (jax.nn.sigmoid(o) * hg_ref[...]).astype(BF16)


def _row_slab_spec(shape, steps):
    rows = shape[-2]
    assert rows % steps == 0 and (rows // steps) % 16 == 0
    block = tuple(shape[:-2]) + (rows // steps, shape[-1])
    lead = (0,) * (len(shape) - 2)
    return pl.BlockSpec(block, lambda i: lead + (i, 0))


def _proj_call(x2, g, w, wkg, gb, head_g, to_cast, tm, batch, seq):
    m, d = x2.shape
    hd = HEADS * DV
    nqk = HEADS * DQK
    assert tm % CHUNK == 0 and seq % tm == 0
    tiles_per_seq = seq // tm
    steps = m // tm
    row = lambda n: pl.BlockSpec((tm, n), lambda i: (i, 0))
    kt_spec = pl.BlockSpec((1, tm // CHUNK, nqk, CHUNK),
                           lambda i: (i // tiles_per_seq, i % tiles_per_seq, 0, 0))
    cast_specs = [_row_slab_spec(a.shape, steps) for a in to_cast]
    outs = pl.pallas_call(
        functools.partial(_proj_kernel, n_cast=len(to_cast)),
        grid=(steps,),
        in_specs=[row(d), _const_spec((1, d)), _const_spec(w.shape), _const_spec(wkg.shape),
                  _const_spec((N_GATES, 1)), _const_spec((1, hd))] + cast_specs,
        out_specs=[row(nqk), kt_spec, row(hd), row(hd),
                   pl.BlockSpec((N_GATES, tm), lambda i: (0, i))] + cast_specs,
        out_shape=[jax.ShapeDtypeStruct((m, nqk), BF16),
                   jax.ShapeDtypeStruct((batch, seq // CHUNK, nqk, CHUNK), BF16),
                   jax.ShapeDtypeStruct((m, hd), BF16),
                   jax.ShapeDtypeStruct((m, hd), BF16),
                   jax.ShapeDtypeStruct((N_GATES, m), F32)]
                  + [jax.ShapeDtypeStruct(a.shape, BF16) for a in to_cast],
        compiler_params=_params(1),
        name="mlstm_proj",
    )(x2, g, w, wkg, gb, head_g, *to_cast)
    return outs[:5], outs[5:]


def _log_sigmoid(z):
    return jnp.minimum(z, 0.0) - jnp.log1p(jnp.exp(-jnp.abs(z)))


def _gate_kernel(gate_ref, triu_ref, ones_ref, tab_ref, row_ref):
    nc = row_ref.shape[1]
    g = gate_ref[...]
    i_f = g[0:HEADS]
    lf_f = _log_sigmoid(g[HEADS:2 * HEADS])
    i_b = g[2 * HEADS:3 * HEADS]
    lf_b = _log_sigmoid(g[3 * HEADS:4 * HEADS])

    def chunk(a, c):
        return a[:, c * CHUNK:(c + 1) * CHUNK]

    stacked = jnp.concatenate([chunk(lf_f, c) for c in range(nc)]
                              + [chunk(lf_b, c) for c in range(nc)], axis=0)
    parts = _split3(stacked)
    cum = _dot3(parts, triu_ref[...])
    tot = _dot3(parts, ones_ref[...])
    pad = jnp.zeros((GATE_LANES - 2 * HEADS, CHUNK), F32)
    for c in range(nc):
        fwd = slice(c * HEADS, (c + 1) * HEADS)
        bwd = slice((nc + c) * HEADS, (nc + c + 1) * HEADS)
        b_f, tot_f = cum[fwd], tot[fwd]
        e_b, tot_b = cum[bwd] - chunk(lf_b, c), tot[bwd]
        a_f = chunk(i_f, c) - b_f
        a_b = chunk(i_b, c) + e_b
        groups = {R_AF: a_f * LOG2_E, R_AB: a_b * LOG2_E, R_TOTB: tot_b * LOG2_E,
                  R_WKF: jnp.exp(tot_f + a_f), R_WKB: jnp.exp(a_b),
                  R_DECF: jnp.exp(tot_f), R_DECB: jnp.exp(tot_b)}
        for r, val in groups.items():
            row_ref[0, c, r * HEADS:(r + 1) * HEADS, :] = val
        cols = jnp.concatenate([b_f * LOG2_E, -e_b * LOG2_E, pad], axis=0)
        tab_ref[c * CHUNK:(c + 1) * CHUNK, :] = cols.T


def _gate_call(gates_t, batch, seq):
    triu = jnp.asarray(np.triu(np.ones((CHUNK, CHUNK), np.float32)), BF16)
    ones = jnp.ones((CHUNK, CHUNK), BF16)
    return pl.pallas_call(
        _gate_kernel,
        grid=(batch,),
        in_specs=[pl.BlockSpec((N_GATES, seq), lambda b: (0, b)),
                  _const_spec((CHUNK, CHUNK)), _const_spec((CHUNK, CHUNK))],
        out_specs=[pl.BlockSpec((seq, GATE_LANES), lambda b: (b, 0)),
                   pl.BlockSpec((1, seq // CHUNK, ROW_TABLE_ROWS, CHUNK), lambda b: (b, 0, 0, 0))],
        out_shape=[jax.ShapeDtypeStruct((batch * seq, GATE_LANES), F32),
                   jax.ShapeDtypeStruct((batch, seq // CHUNK, ROW_TABLE_ROWS, CHUNK), F32)],
        compiler_params=_params(1),
        name="mlstm_gates",
    )(gates_t, triu, ones)


def _mlstm_kernel(q_ref, kt_ref, v_ref, og_ref, tab_ref, row_ref, out_ref,
                  u_ref, sf_ref, sb_ref):
    nc = kt_ref.shape[1]
    ones_blk = jnp.ones((CHUNK, DV), BF16)
    zero_k = jnp.zeros((DQK, CHUNK), BF16)
    zero_state = jnp.zeros((DQK, 2 * DV), BF16)
    zero_q = jnp.zeros((CHUNK, 2 * DQK), BF16)
    t_idx = lax.broadcasted_iota(jnp.int32, (CHUNK, CHUNK), 0)
    s_idx = lax.broadcasted_iota(jnp.int32, (CHUNK, CHUNK), 1)
    lower = s_idx <= t_idx
    upper = s_idx >= t_idx

    def head_steps(h):
        pair_lanes = slice((h // 2) * 2 * DQK, (h // 2 + 1) * 2 * DQK)
        k_rows = slice(h * DQK, (h + 1) * DQK)
        v_lanes = slice(h * DV, (h + 1) * DV)

        def pad_rows(a, zero, h=h):
            return jnp.concatenate([a, zero] if h % 2 == 0 else [zero, a], axis=0)

        def chunk_rows(c):
            return pl.ds(pl.multiple_of(c * CHUNK, CHUNK), CHUNK)

        def vaug(rows, v_lanes=v_lanes):
            return jnp.concatenate([v_ref[rows, v_lanes], ones_blk], axis=1)

        def row(c, group, h=h):
            return row_ref[0, c, group * HEADS + h:group * HEADS + h + 1, :]

        def col(tab, group, h=h):
            return tab[:, group * HEADS + h:group * HEADS + h + 1]

        def contrib_step(c, carry):
            kt = kt_ref[0, c, k_rows, :].astype(F32)
            kw = jnp.concatenate([(kt * row(c, R_WKF)).astype(BF16),
                                  (kt * row(c, R_WKB)).astype(BF16)], axis=0)
            u_ref[c] = _dot(kw, vaug(chunk_rows(c)))
            return carry

        def chunk_scalar(c, group, h=h):
            return row_ref[0, c, group * HEADS + h:group * HEADS + h + 1, 0:1]

        def run_scans():
            state = jnp.zeros((DQK, 2 * DV), F32)
            for c in range(nc):
                sf_ref[c] = state.astype(BF16)
                state = chunk_scalar(c, R_DECF) * state + u_ref[c, 0:DQK, :]
            state = jnp.zeros((DQK, 2 * DV), F32)
            for c in reversed(range(nc)):
                sb_ref[c] = state.astype(BF16)
                state = chunk_scalar(c, R_DECB) * state + u_ref[c, DQK:2 * DQK, :]

        def out_step(c, carry):
            rows = chunk_rows(c)
            q = q_ref[rows, pair_lanes]
            s = _dot(q, pad_rows(kt_ref[0, c, k_rows, :], zero_k))
            tab = tab_ref[rows, :]
            c_f = jnp.broadcast_to(col(tab, G_CF), (CHUNK, CHUNK))
            c_b = jnp.broadcast_to(col(tab, G_CB), (CHUNK, CHUNK))
            d_f = jnp.exp2(jnp.where(lower, c_f + row(c, R_AF), -jnp.inf))
            d_b = jnp.exp2(jnp.where(upper, c_b + row(c, R_AB), -jnp.inf))
            qf32 = q.astype(F32)
            q_f = (qf32 * jnp.exp2(c_f[:, :2 * DQK])).astype(BF16)
            q_b = (qf32 * jnp.exp2(c_b[:, :2 * DQK] + chunk_scalar(c, R_TOTB))).astype(BF16)
            lhs = jnp.concatenate(
                [jnp.concatenate([(s * d_f).astype(BF16), q_f, zero_q], axis=1),
                 jnp.concatenate([(s * d_b).astype(BF16), zero_q, q_b], axis=1)], axis=0)
            rhs = jnp.concatenate([vaug(rows), pad_rows(sf_ref[c], zero_state),
                                   pad_rows(sb_ref[c], zero_state)], axis=0)
            nd = _dot(lhs, rhs)
            nd_f = nd[:CHUNK]
            nd_b = nd[CHUNK:]
            hh = (nd_f[:, :DV] / jnp.maximum(jnp.abs(nd_f[:, DV:]), 1.0)
                  + nd_b[:, :DV] / jnp.maximum(jnp.abs(nd_b[:, DV:]), 1.0))
            hn = hh * lax.rsqrt(jnp.mean(hh * hh, axis=-1, keepdims=True) + EPS)
            gated = hn * og_ref[rows, v_lanes].astype(F32)
            out_ref[rows, v_lanes] = gated.astype(BF16)
            return carry

        return contrib_step, run_scans, out_step

    steps = [head_steps(h) for h in range(HEADS)]
    lax.fori_loop(0, nc, steps[0][0], 0, unroll=MLSTM_UNROLL)
    steps[0][1]()
    for h in range(HEADS):
        out_step = steps[h][2]
        if h + 1 < HEADS:
            next_contrib = steps[h + 1][0]
            lax.fori_loop(0, nc, lambda c, carry: next_contrib(c, out_step(c, carry)), 0,
                          unroll=MLSTM_UNROLL)
            steps[h + 1][1]()
        else:
            lax.fori_loop(0, nc, out_step, 0, unroll=MLSTM_UNROLL)


def _mlstm_call(q, kt, v, og, tab, rowtab, batch, seq):
    hd = HEADS * DV
    nqk = HEADS * DQK
    nc = seq // CHUNK
    seq_blk = lambda n: pl.BlockSpec((seq, n), lambda b: (b, 0))
    return pl.pallas_call(
        _mlstm_kernel,
        grid=(batch,),
        in_specs=[seq_blk(nqk), pl.BlockSpec((1, nc, nqk, CHUNK), lambda b: (b, 0, 0, 0)),
                  seq_blk(hd), seq_blk(hd), seq_blk(GATE_LANES),
                  pl.BlockSpec((1, nc, ROW_TABLE_ROWS, CHUNK), lambda b: (b, 0, 0, 0))],
        out_specs=seq_blk(hd),
        out_shape=jax.ShapeDtypeStruct((batch * seq, hd), BF16),
        scratch_shapes=[pltpu.VMEM((nc, 2 * DQK, 2 * DV), F32),
                        pltpu.VMEM((nc, DQK, 2 * DV), BF16), pltpu.VMEM((nc, DQK, 2 * DV), BF16)],
        compiler_params=_params(1),
        name="mlstm_core",
    )(q, kt, v, og, tab, rowtab)


def _mlp_kernel(*refs, has_scale, final_norm, ff_chunk):
    refs = list(refs)
    x_ref, a_ref, wo_ref = refs[:3]
    del refs[:3]
    sc_ref = refs.pop(0) if has_scale else None
    g_ref, w1_ref, w2_ref = refs[:3]
    del refs[:3]
    fg_ref = refs.pop(0) if final_norm else None
    (out_ref,) = refs

    r = _dot(a_ref[...], wo_ref[...])
    if has_scale:
        r = r * sc_ref[...]
    x1 = x_ref[...] + r
    u = _rms(x1, g_ref[...]).astype(BF16)
    y = x1
    for c in range(w1_ref.shape[1] // ff_chunk):
        cols = slice(c * ff_chunk, (c + 1) * ff_chunk)
        hid = jnp.square(jnp.maximum(_dot(u, w1_ref[:, cols]), 0.0)).astype(BF16)
        y = y + _dot(hid, w2_ref[cols, :])
    if final_norm:
        y = _rms(y, fg_ref[...])
    out_ref[...] = y


def _layer_spec(shape, layer):
    zeros = (0,) * (len(shape) - 1)
    return pl.BlockSpec((None,) + tuple(shape[1:]), lambda *_: (layer,) + zeros,
                        pipeline_mode=pl.Buffered(1))


def _mlp_call(x2, a, wo, scale, g, w1, w2, layer, final_g, tm, ff_chunk=1024):
    m, d = x2.shape
    row = lambda n: pl.BlockSpec((tm, n), lambda i: (i, 0))
    args = [x2, a, wo]
    specs = [row(d), row(a.shape[1]), _const_spec(wo.shape)]
    if scale is not None:
        args.append(scale)
        specs.append(_const_spec((1, d)))
    args += [g, w1, w2]
    specs += [_const_spec((1, d)), _layer_spec(w1.shape, layer), _layer_spec(w2.shape, layer)]
    if final_g is not None:
        args.append(final_g)
        specs.append(_const_spec((1, d)))
    body = functools.partial(_mlp_kernel, has_scale=scale is not None,
                             final_norm=final_g is not None, ff_chunk=ff_chunk)
    return pl.pallas_call(
        body,
        grid=(m // tm,),
        in_specs=specs,
        out_specs=row(d),
        out_shape=jax.ShapeDtypeStruct((m, d), F32),
        compiler_params=_params(1),
        name="mlp",
    )(*args)


def _pool_kernel(x_ref, g_ref, win_ref, wgrp_ref, out_ref, a_scr, *, proj_rows, pool_rows):
    seq, d = x_ref.shape
    gdim = d // len(POOL_WINDOWS)
    a_scr[0:POOL_HALO, :] = jnp.zeros((POOL_HALO, d), F32)
    a_scr[POOL_HALO + seq:, :] = jnp.zeros((POOL_HALO, d), F32)
    for t in range(seq // proj_rows):
        rows = slice(t * proj_rows, (t + 1) * proj_rows)
        u = _rms(x_ref[rows, :], g_ref[...]).astype(BF16)
        a_scr[POOL_HALO + t * proj_rows:POOL_HALO + (t + 1) * proj_rows, :] = _dot(u, win_ref[...])

    n_tiles = seq // pool_rows
    n = pool_rows + 2 * POOL_HALO
    centre = slice(POOL_HALO, POOL_HALO + pool_rows)
    for t in range(n_tiles):
        t0 = t * pool_rows
        for gi, w in enumerate(POOL_WINDOWS):
            cols = slice(gi * gdim, (gi + 1) * gdim)
            xw = a_scr[t0:t0 + n, cols]
            fwd = xw
            span = 1
            while 2 * span < w:
                fwd = fwd + pltpu.roll(fwd, n - span, axis=0)
                span *= 2
            s = fwd + pltpu.roll(fwd, w // 2, axis=0)
            if 0 < t < n_tiles - 1:
                mean = s[centre, :] * (1.0 / w)
            else:
                pos = t0 + lax.broadcasted_iota(jnp.int32, (pool_rows, gdim), 0)
                count = jnp.minimum(pos + w // 2, seq) - jnp.maximum(pos - w // 2, 0)
                mean = s[centre, :] / count.astype(F32)
            pooled = mean - xw[centre, :]
            mixed = _dot(pooled.astype(BF16), wgrp_ref[gi])
            out_ref[t0:t0 + pool_rows, cols] = mixed.astype(BF16)


def _pool_call(x2, g, w_in, w_group, batch, seq, proj_rows=512, pool_rows=256):
    d = x2.shape[1]
    seq_blk = pl.BlockSpec((seq, d), lambda b: (b, 0))
    body = functools.partial(_pool_kernel, proj_rows=proj_rows, pool_rows=pool_rows)
    return pl.pallas_call(
        body,
        grid=(batch,),
        in_specs=[seq_blk, _const_spec((1, d)), _const_spec(w_in.shape), _const_spec(w_group.shape)],
        out_specs=seq_blk,
        out_shape=jax.ShapeDtypeStruct((batch * seq, d), BF16),
        scratch_shapes=[pltpu.VMEM((seq + 2 * POOL_HALO, d), F32)],
        compiler_params=_params(1),
        name="pool_mixer",
    )(x2, g, w_in, w_group)


def kernel(x, mix_norm_g, mlp_norm_g, mlstm_w_in, mlstm_gate_b, mlstm_head_g, mlstm_w_out,
           pool_w_in, pool_w_group, pool_w_out, pool_scale, mlp_w1, mlp_w2, final_norm_g):
    batch, seq, d = x.shape
    assert seq % CHUNK == 0 and d == HEADS * DV
    m = batch * seq
    tm = 1024
    x2 = x.reshape(m, d)
    vec = lambda a: a.reshape(1, -1).astype(F32)

    nqk = HEADS * DQK
    w_main = mlstm_w_in[0]
    wkg = jnp.concatenate([w_main[:, nqk:2 * nqk], w_main[:, 2 * nqk + 2 * d:]],
                          axis=1).astype(BF16)
    gb = mlstm_gate_b[0].astype(F32).reshape(N_GATES, 1)
    n_groups, gdim = pool_w_group.shape[1], pool_w_group.shape[2]

    later_f32 = [mlp_w1, mlp_w2, mlstm_w_out[0], pool_w_in[0], pool_w_out[0],
                 pool_w_group[0].reshape(n_groups * gdim, gdim)]
    (q, kt, v, og, gates_t), (w1, w2, w_out, p_in, p_out, p_group) = _proj_call(
        x2, vec(mix_norm_g[0]), w_main, wkg, gb, vec(mlstm_head_g[0]), later_f32, 1024, batch, seq)
    tab, rowtab = _gate_call(gates_t, batch, seq)
    hmix = _mlstm_call(q, kt, v, og, tab, rowtab, batch, seq)
    x2 = _mlp_call(x2, hmix, w_out, None, vec(mlp_norm_g[0]), w1, w2, 0, None, tm)

    mixed = _pool_call(x2, vec(mix_norm_g[1]), p_in, p_group.reshape(n_groups, gdim, gdim),
                       batch, seq)
    x2 = _mlp_call(x2, mixed, p_out, vec(pool_scale[0]), vec(mlp_norm_g[1]),
                   w1, w2, 1, vec(final_norm_g), tm)
    return x2.reshape(batch, seq, d)
```

```python
import functools

import numpy as np
import jax
import jax.numpy as jnp
from jax import lax
from jax.experimental import pallas as pl
from jax.experimental.pallas import tpu as pltpu

F32 = jnp.float32
BF16 = jnp.bfloat16

HEADS = 8
DQK = 64
DV = 128
CHUNK = 256
HALF = CHUNK // 2
LO, HI = slice(0, HALF), slice(HALF, CHUNK)
MLSTM_UNROLL = 8
GATE_SOFTCAP = 15.0
POOL_WINDOWS = (2, 4, 8, 16)
POOL_HALO = 8
EPS = 1e-6
LOG2_E = 1.4426950408889634
GATE_LANES = 128
VMEM_LIMIT = 56 * 1024 * 1024

N_GATES = 4 * HEADS
G_CF, G_CB = range(2)
R_AF, R_AB, R_TOTB, R_WKF, R_WKB, R_DECF, R_DECB = range(7)
ROW_TABLE_ROWS = 7 * HEADS


def _rms(x, g):
    ms = jnp.mean(x * x, axis=-1, keepdims=True)
    return x * lax.rsqrt(ms + EPS) * g


def _dot(a, b):
    return jnp.dot(a, b, preferred_element_type=F32)


def _dot_nt(a, b):
    return lax.dot_general(a, b, (((1,), (1,)), ((), ())), preferred_element_type=F32)


def _split3(x):
    hi = x.astype(BF16)
    r1 = x - hi.astype(F32)
    mid = r1.astype(BF16)
    lo = (r1 - mid.astype(F32)).astype(BF16)
    return hi, mid, lo


def _dot3(parts, m):
    return _dot(parts[0], m) + _dot(parts[1], m) + _dot(parts[2], m)


def _const_spec(shape):
    zeros = (0,) * len(shape)
    return pl.BlockSpec(shape, lambda *_: zeros, pipeline_mode=pl.Buffered(1))


def _params(n_axes):
    return pltpu.CompilerParams(dimension_semantics=("arbitrary",) * n_axes,
                                vmem_limit_bytes=VMEM_LIMIT)


def _log_sigmoid(z):
    return jnp.minimum(z, 0.0) - jnp.log1p(jnp.exp(-jnp.abs(z)))


def _gate_tables(g, triu, ones, tab_ref, row_ref):
    nc = g.shape[1] // CHUNK
    i_f = g[0:HEADS]
    lf_f = _log_sigmoid(g[HEADS:2 * HEADS])
    i_b = g[2 * HEADS:3 * HEADS]
    lf_b = _log_sigmoid(g[3 * HEADS:4 * HEADS])

    def chunk(a, c):
        return a[:, c * CHUNK:(c + 1) * CHUNK]

    stacked = jnp.concatenate([chunk(lf_f, c) for c in range(nc)]
                              + [chunk(lf_b, c) for c in range(nc)], axis=0)
    parts = _split3(stacked)
    cum = _dot3(parts, triu)
    tot = _dot3(parts, ones)
    pad = jnp.zeros((GATE_LANES - 2 * HEADS, CHUNK), F32)
    for c in range(nc):
        fwd = slice(c * HEADS, (c + 1) * HEADS)
        bwd = slice((nc + c) * HEADS, (nc + c + 1) * HEADS)
        b_f, tot_f = cum[fwd], tot[fwd]
        e_b, tot_b = cum[bwd] - chunk(lf_b, c), tot[bwd]
        a_f = chunk(i_f, c) - b_f
        a_b = chunk(i_b, c) + e_b
        groups = {R_AF: a_f * LOG2_E, R_AB: a_b * LOG2_E, R_TOTB: tot_b * LOG2_E,
                  R_WKF: jnp.exp(tot_f + a_f), R_WKB: jnp.exp(a_b),
                  R_DECF: jnp.exp(tot_f), R_DECB: jnp.exp(tot_b)}
        for r, val in groups.items():
            row_ref[0, c, r * HEADS:(r + 1) * HEADS, :] = val
        cols = jnp.concatenate([b_f * LOG2_E, -e_b * LOG2_E, pad], axis=0)
        tab_ref[c * CHUNK:(c + 1) * CHUNK, :] = cols.T


def _proj_kernel(*refs, n_cast):
    x_ref, g_ref, wt_ref, gb_ref, hg_ref, triu_ref, ones_ref = refs[:7]
    cast_in = refs[7:7 + n_cast]
    q_ref, kt_ref, v_ref, og_ref, tab_ref, row_ref = refs[7 + n_cast:13 + n_cast]
    cast_out = refs[13 + n_cast:]
    for src, dst in zip(cast_in, cast_out):
        dst[...] = src[...].astype(BF16)
    nqk = HEADS * DQK
    hd = HEADS * DV

    def w_rows(lo, hi):
        return wt_ref[lo:hi, :].astype(BF16)

    w_q = w_rows(0, nqk)
    w_kg = jnp.concatenate([w_rows(nqk, 2 * nqk),
                            w_rows(2 * nqk + 2 * hd, 2 * nqk + 2 * hd + N_GATES)], axis=0)
    w_v = w_rows(2 * nqk, 2 * nqk + hd)
    w_o = w_rows(2 * nqk + hd, 2 * nqk + 2 * hd)
    u = _rms(x_ref[...], g_ref[...]).astype(BF16)
    ktg = _dot_nt(w_kg, u)
    kt = ktg[0:nqk].astype(BF16)
    for c in range(kt_ref.shape[1]):
        kt_ref[0, c] = kt[:, c * CHUNK:(c + 1) * CHUNK]
    g = ktg[nqk:nqk + N_GATES] + gb_ref[...]
    g = GATE_SOFTCAP * jnp.tanh(g / GATE_SOFTCAP)
    _gate_tables(g, triu_ref[...], ones_ref[...], tab_ref, row_ref)
    o = _dot_nt(u, w_o)
    og_ref[...] = (jax.nn.sigmoid(o) * hg_ref[...]).astype(BF16)
    v_ref[...] = _dot_nt(u, w_v).astype(BF16)
    q_ref[...] = (_dot_nt(u, w_q) * (DQK ** -0.5)).astype(BF16)


def _row_slab_spec(shape, steps):
    rows = shape[-2]
    assert rows % steps == 0 and (rows // steps) % 16 == 0
    block = tuple(shape[:-2]) + (rows // steps, shape[-1])
    lead = (0,) * (len(shape) - 2)
    return pl.BlockSpec(block, lambda i: lead + (i, 0))


def _proj_call(x2, g, wt, gb, head_g, to_cast, tm, batch, seq):
    m, d = x2.shape
    hd = HEADS * DV
    nqk = HEADS * DQK
    assert tm % CHUNK == 0 and seq % tm == 0
    tiles_per_seq = seq // tm
    steps = m // tm
    row = lambda n: pl.BlockSpec((tm, n), lambda i: (i, 0))
    chunk_major = lambda n: pl.BlockSpec((1, tm // CHUNK, n, CHUNK),
                                         lambda i: (i // tiles_per_seq, i % tiles_per_seq, 0, 0))
    cast_specs = [_row_slab_spec(a.shape, steps) for a in to_cast]
    triu = jnp.asarray(np.triu(np.ones((CHUNK, CHUNK), np.float32)), BF16)
    ones = jnp.asarray(np.ones((CHUNK, CHUNK), np.float32), BF16)
    outs = pl.pallas_call(
        functools.partial(_proj_kernel, n_cast=len(to_cast)),
        grid=(steps,),
        in_specs=[row(d), _const_spec((1, d)), _const_spec(wt.shape),
                  _const_spec((N_GATES, 1)), _const_spec((1, hd)),
                  _const_spec((CHUNK, CHUNK)), _const_spec((CHUNK, CHUNK))] + cast_specs,
        out_specs=[row(nqk), chunk_major(nqk), row(hd), row(hd), row(GATE_LANES),
                   chunk_major(ROW_TABLE_ROWS)] + cast_specs,
        out_shape=[jax.ShapeDtypeStruct((m, nqk), BF16),
                   jax.ShapeDtypeStruct((batch, seq // CHUNK, nqk, CHUNK), BF16),
                   jax.ShapeDtypeStruct((m, hd), BF16),
                   jax.ShapeDtypeStruct((m, hd), BF16),
                   jax.ShapeDtypeStruct((m, GATE_LANES), F32),
                   jax.ShapeDtypeStruct((batch, seq // CHUNK, ROW_TABLE_ROWS, CHUNK), F32)]
                  + [jax.ShapeDtypeStruct(a.shape, BF16) for a in to_cast],
        compiler_params=_params(1),
        name="mlstm_proj",
    )(x2, g, wt, gb, head_g, triu, ones, *to_cast)
    return outs[:6], outs[6:]


def _mlstm_kernel(q_ref, kt_ref, v_ref, og_ref, tab_ref, row_ref, out_ref,
                  u_ref, sf_ref, sb_ref):
    nc = kt_ref.shape[1]
    ones_blk = jnp.ones((CHUNK, DV), BF16)
    zero_k = jnp.zeros((DQK, CHUNK), BF16)
    zero_state = jnp.zeros((DQK, 2 * DV), BF16)
    zero_q = jnp.zeros((HALF, 2 * DQK), BF16)
    t_idx = lax.broadcasted_iota(jnp.int32, (HALF, HALF), 0)
    s_idx = lax.broadcasted_iota(jnp.int32, (HALF, HALF), 1)
    lower = s_idx <= t_idx
    upper = s_idx >= t_idx

    def head_steps(h):
        pair_lanes = slice((h // 2) * 2 * DQK, (h // 2 + 1) * 2 * DQK)
        k_rows = slice(h * DQK, (h + 1) * DQK)
        v_lanes = slice(h * DV, (h + 1) * DV)

        def pad_rows(a, zero, h=h):
            return jnp.concatenate([a, zero] if h % 2 == 0 else [zero, a], axis=0)

        def chunk_rows(c):
            return pl.ds(pl.multiple_of(c * CHUNK, CHUNK), CHUNK)

        def vaug(rows, v_lanes=v_lanes):
            return jnp.concatenate([v_ref[rows, v_lanes], ones_blk], axis=1)

        def row(c, group, h=h):
            return row_ref[0, c, group * HEADS + h:group * HEADS + h + 1, :]

        def col(tab, group, h=h):
            return tab[:, group * HEADS + h:group * HEADS + h + 1]

        def contrib_step(c, carry):
            kt = kt_ref[0, c, k_rows, :].astype(F32)
            kw = jnp.concatenate([(kt * row(c, R_WKF)).astype(BF16),
                                  (kt * row(c, R_WKB)).astype(BF16)], axis=0)
            u_ref[c] = _dot(kw, vaug(chunk_rows(c)))
            return carry

        def chunk_scalar(c, group, h=h):
            return row_ref[0, c, group * HEADS + h:group * HEADS + h + 1, 0:1]

        def run_scans():
            state = jnp.zeros((DQK, 2 * DV), F32)
            for c in range(nc):
                sf_ref[c] = state.astype(BF16)
                state = chunk_scalar(c, R_DECF) * state + u_ref[c, 0:DQK, :]
            state = jnp.zeros((DQK, 2 * DV), F32)
            for c in reversed(range(nc)):
                sb_ref[c] = state.astype(BF16)
                state = chunk_scalar(c, R_DECB) * state + u_ref[c, DQK:2 * DQK, :]

        def out_step(c, carry):
            rows = chunk_rows(c)
            q = q_ref[rows, pair_lanes]
            s = _dot(q, pad_rows(kt_ref[0, c, k_rows, :], zero_k))
            tab = tab_ref[rows, :]
            c_f = jnp.broadcast_to(col(tab, G_CF), (CHUNK, HALF))
            c_b = jnp.broadcast_to(col(tab, G_CB), (CHUNK, HALF))
            a_f = row(c, R_AF)
            a_b = row(c, R_AB)

            def blk(t_half, s_half, c_col, a_row, mask=None):
                e = c_col[t_half] + a_row[:, s_half]
                if mask is not None:
                    e = jnp.where(mask, e, -jnp.inf)
                return (s[t_half, s_half] * jnp.exp2(e)).astype(BF16)

            qf32 = q.astype(F32)
            q_f = (qf32 * jnp.exp2(c_f[:, :2 * DQK])).astype(BF16)
            q_b = (qf32 * jnp.exp2(c_b[:, :2 * DQK] + chunk_scalar(c, R_TOTB))).astype(BF16)
            va = vaug(rows)
            rhs_lo = jnp.concatenate([va[LO], pad_rows(sf_ref[c], zero_state)], axis=0)
            rhs_hi = jnp.concatenate([va[HI], pad_rows(sb_ref[c], zero_state)], axis=0)
            zq = zero_q
            lhs_lo = jnp.concatenate(
                [jnp.concatenate([blk(LO, LO, c_f, a_f, lower), q_f[LO]], axis=1),
                 jnp.concatenate([blk(HI, LO, c_f, a_f), q_f[HI]], axis=1),
                 jnp.concatenate([blk(LO, LO, c_b, a_b, upper), zq], axis=1)], axis=0)
            lhs_hi = jnp.concatenate(
                [jnp.concatenate([blk(HI, HI, c_f, a_f, lower), zq], axis=1),
                 jnp.concatenate([blk(LO, HI, c_b, a_b), q_b[LO]], axis=1),
                 jnp.concatenate([blk(HI, HI, c_b, a_b, upper), q_b[HI]], axis=1)], axis=0)
            r_lo = _dot(lhs_lo, rhs_lo)
            r_hi = _dot(lhs_hi, rhs_hi)
            nd_f = jnp.concatenate([r_lo[0:HALF], r_lo[HALF:2 * HALF] + r_hi[0:HALF]], axis=0)
            nd_b = jnp.concatenate([r_lo[2 * HALF:] + r_hi[HALF:2 * HALF], r_hi[2 * HALF:]], axis=0)
            hh = (nd_f[:, :DV] / jnp.maximum(jnp.abs(nd_f[:, DV:]), 1.0)
                  + nd_b[:, :DV] / jnp.maximum(jnp.abs(nd_b[:, DV:]), 1.0))
            hn = hh * lax.rsqrt(jnp.mean(hh * hh, axis=-1, keepdims=True) + EPS)
            gated = hn * og_ref[rows, v_lanes].astype(F32)
            out_ref[rows, v_lanes] = gated.astype(BF16)
            return carry

        return contrib_step, run_scans, out_step

    steps = [head_steps(h) for h in range(HEADS)]
    lax.fori_loop(0, nc, steps[0][0], 0, unroll=MLSTM_UNROLL)
    steps[0][1]()
    for h in range(HEADS):
        out_step = steps[h][2]
        if h + 1 < HEADS:
            next_contrib = steps[h + 1][0]
            lax.fori_loop(0, nc, lambda c, carry: next_contrib(c, out_step(c, carry)), 0,
                          unroll=MLSTM_UNROLL)
            steps[h + 1][1]()
        else:
            lax.fori_loop(0, nc, out_step, 0, unroll=MLSTM_UNROLL)


def _mlstm_call(q, kt, v, og, tab, rowtab, batch, seq):
    hd = HEADS * DV
    nqk = HEADS * DQK
    nc = seq // CHUNK
    seq_blk = lambda n: pl.BlockSpec((seq, n), lambda b: (b, 0))
    return pl.pallas_call(
        _mlstm_kernel,
        grid=(batch,),
        in_specs=[seq_blk(nqk), pl.BlockSpec((1, nc, nqk, CHUNK), lambda b: (b, 0, 0, 0)),
                  seq_blk(hd), seq_blk(hd), seq_blk(GATE_LANES),
                  pl.BlockSpec((1, nc, ROW_TABLE_ROWS, CHUNK), lambda b: (b, 0, 0, 0))],
        out_specs=seq_blk(hd),
        out_shape=jax.ShapeDtypeStruct((batch * seq, hd), BF16),
        scratch_shapes=[pltpu.VMEM((nc, 2 * DQK, 2 * DV), F32),
                        pltpu.VMEM((nc, DQK, 2 * DV), BF16), pltpu.VMEM((nc, DQK, 2 * DV), BF16)],
        compiler_params=_params(1),
        name="mlstm_core",
    )(q, kt, v, og, tab, rowtab)


def _mlp_kernel(*refs, has_scale, out_norm, ff_chunk):
    refs = list(refs)
    x_ref, a_ref, wo_ref = refs[:3]
    del refs[:3]
    sc_ref = refs.pop(0) if has_scale else None
    g_ref, w1_ref, w2_ref = refs[:3]
    del refs[:3]
    fg_ref = refs.pop(0) if out_norm else None
    out_ref = refs.pop(0)

    r = _dot(a_ref[...], wo_ref[...])
    if has_scale:
        r = r * sc_ref[...]
    x1 = x_ref[...] + r
    u = _rms(x1, g_ref[...]).astype(BF16)
    y = x1
    for c in range(w1_ref.shape[1] // ff_chunk):
        cols = slice(c * ff_chunk, (c + 1) * ff_chunk)
        hid = jnp.square(jnp.maximum(_dot(u, w1_ref[:, cols]), 0.0)).astype(BF16)
        y = y + _dot(hid, w2_ref[cols, :])
    if out_norm == "final":
        y = _rms(y, fg_ref[...])
    elif out_norm == "next":
        (u_next_ref,) = refs
        u_next_ref[...] = _rms(y, fg_ref[...]).astype(BF16)
    out_ref[...] = y


def _layer_spec(shape, layer):
    zeros = (0,) * (len(shape) - 1)
    return pl.BlockSpec((None,) + tuple(shape[1:]), lambda *_: (layer,) + zeros,
                        pipeline_mode=pl.Buffered(1))


def _mlp_call(x2, a, wo, scale, g, w1, w2, layer, out_g, out_norm, tm, ff_chunk=1024):
    m, d = x2.shape
    row = lambda n: pl.BlockSpec((tm, n), lambda i: (i, 0))
    args = [x2, a, wo]
    specs = [row(d), row(a.shape[1]), _const_spec(wo.shape)]
    if scale is not None:
        args.append(scale)
        specs.append(_const_spec((1, d)))
    args += [g, w1, w2]
    specs += [_const_spec((1, d)), _layer_spec(w1.shape, layer), _layer_spec(w2.shape, layer)]
    assert out_norm in ("final", "next")
    args.append(out_g)
    specs.append(_const_spec((1, d)))
    out_specs, out_shape = [row(d)], [jax.ShapeDtypeStruct((m, d), F32)]
    if out_norm == "next":
        out_specs.append(row(d))
        out_shape.append(jax.ShapeDtypeStruct((m, d), BF16))
    body = functools.partial(_mlp_kernel, has_scale=scale is not None, out_norm=out_norm,
                             ff_chunk=ff_chunk)
    outs = pl.pallas_call(
        body,
        grid=(m // tm,),
        in_specs=specs,
        out_specs=out_specs,
        out_shape=out_shape,
        compiler_params=_params(1),
        name="mlp",
    )(*args)
    return outs[0] if out_norm == "final" else tuple(outs)


def _pool_kernel(u_ref, win_ref, wgrp_ref, out_ref, a_scr, *, proj_rows, pool_rows):
    seq, d = u_ref.shape
    gdim = d // len(POOL_WINDOWS)
    a_scr[0:POOL_HALO, :] = jnp.zeros((POOL_HALO, d), F32)
    a_scr[POOL_HALO + seq:, :] = jnp.zeros((POOL_HALO, d), F32)
    for t in range(seq // proj_rows):
        rows = slice(t * proj_rows, (t + 1) * proj_rows)
        a_scr[POOL_HALO + t * proj_rows:POOL_HALO + (t + 1) * proj_rows, :] = _dot(u_ref[rows, :],
                                                                                   win_ref[...])

    n_tiles = seq // pool_rows
    n = pool_rows + 2 * POOL_HALO
    centre = slice(POOL_HALO, POOL_HALO + pool_rows)
    for t in range(n_tiles):
        t0 = t * pool_rows
        for gi, w in enumerate(POOL_WINDOWS):
            cols = slice(gi * gdim, (gi + 1) * gdim)
            xw = a_scr[t0:t0 + n, cols]
            fwd = xw
            span = 1
            while 2 * span < w:
                fwd = fwd + pltpu.roll(fwd, n - span, axis=0)
                span *= 2
            s = fwd + pltpu.roll(fwd, w // 2, axis=0)
            if 0 < t < n_tiles - 1:
                mean = s[centre, :] * (1.0 / w)
            else:
                pos = t0 + lax.broadcasted_iota(jnp.int32, (pool_rows, gdim), 0)
                count = jnp.minimum(pos + w // 2, seq) - jnp.maximum(pos - w // 2, 0)
                mean = s[centre, :] / count.astype(F32)
            pooled = mean - xw[centre, :]
            mixed = _dot(pooled.astype(BF16), wgrp_ref[gi])
            out_ref[t0:t0 + pool_rows, cols] = mixed.astype(BF16)


def _pool_call(u, w_in, w_group, batch, seq, proj_rows=512, pool_rows=256):
    d = u.shape[1]
    seq_blk = pl.BlockSpec((seq, d), lambda b: (b, 0))
    body = functools.partial(_pool_kernel, proj_rows=proj_rows, pool_rows=pool_rows)
    return pl.pallas_call(
        body,
        grid=(batch,),
        in_specs=[seq_blk, _const_spec(w_in.shape), _const_spec(w_group.shape)],
        out_specs=seq_blk,
        out_shape=jax.ShapeDtypeStruct((batch * seq, d), BF16),
        scratch_shapes=[pltpu.VMEM((seq + 2 * POOL_HALO, d), F32)],
        compiler_params=_params(1),
        name="pool_mixer",
    )(u, w_in, w_group)


def kernel(x, mix_norm_g, mlp_norm_g, mlstm_w_in, mlstm_gate_b, mlstm_head_g, mlstm_w_out,
           pool_w_in, pool_w_group, pool_w_out, pool_scale, mlp_w1, mlp_w2, final_norm_g):
    batch, seq, d = x.shape
    assert seq % CHUNK == 0 and d == HEADS * DV and HALF == 2 * DQK
    m = batch * seq
    tm = 1024
    x2 = x.reshape(m, d)
    vec = lambda a: a.reshape(1, -1).astype(F32)

    w_in_t = mlstm_w_in[0].T
    gb = mlstm_gate_b[0].astype(F32).reshape(N_GATES, 1)
    n_groups, gdim = pool_w_group.shape[1], pool_w_group.shape[2]

    later_f32 = [mlp_w1, mlp_w2, mlstm_w_out[0], pool_w_in[0], pool_w_out[0],
                 pool_w_group[0].reshape(n_groups * gdim, gdim)]
    (q, kt, v, og, tab, rowtab), (w1, w2, w_out, p_in, p_out, p_group) = _proj_call(
        x2, vec(mix_norm_g[0]), w_in_t, gb, vec(mlstm_head_g[0]), later_f32, tm, batch, seq)
    hmix = _mlstm_call(q, kt, v, og, tab, rowtab, batch, seq)
    x2, u_pool = _mlp_call(x2, hmix, w_out, None, vec(mlp_norm_g[0]), w1, w2, 0,
                           vec(mix_norm_g[1]), "next", tm)

    mixed = _pool_call(u_pool, p_in, p_group.reshape(n_groups, gdim, gdim), batch, seq)
    x2 = _mlp_call(x2, mixed, p_out, vec(pool_scale[0]), vec(mlp_norm_g[1]),
                   w1, w2, 1, vec(final_norm_g), "final", tm)
    return x2.reshape(batch, seq, d)
```

```python
import functools

import numpy as np
import jax
import jax.numpy as jnp
from jax import lax
from jax.experimental import pallas as pl
from jax.experimental.pallas import tpu as pltpu

F32 = jnp.float32
BF16 = jnp.bfloat16

HEADS = 8
DQK = 64
DV = 128
CHUNK = 256
HALF = CHUNK // 2
LO, HI = slice(0, HALF), slice(HALF, CHUNK)
MLSTM_UNROLL = 8
GATE_SOFTCAP = 15.0
POOL_WINDOWS = (2, 4, 8, 16)
POOL_HALO = 8
EPS = 1e-6
LOG2_E = 1.4426950408889634
GATE_LANES = 128
VMEM_LIMIT = 56 * 1024 * 1024

N_GATES = 4 * HEADS
G_CF, G_CB = range(2)
R_AF, R_AB, R_TOTB, R_WKF, R_WKB, R_DECF, R_DECB = range(7)
ROW_TABLE_ROWS = 7 * HEADS


def _rms(x, g):
    ms = jnp.mean(x * x, axis=-1, keepdims=True)
    return x * lax.rsqrt(ms + EPS) * g


def _dot(a, b):
    return jnp.dot(a, b, preferred_element_type=F32)


def _dot_nt(a, b):
    return lax.dot_general(a, b, (((1,), (1,)), ((), ())), preferred_element_type=F32)


def _split3(x):
    hi = x.astype(BF16)
    r1 = x - hi.astype(F32)
    mid = r1.astype(BF16)
    lo = (r1 - mid.astype(F32)).astype(BF16)
    return hi, mid, lo


def _dot3(parts, m):
    return _dot(parts[0], m) + _dot(parts[1], m) + _dot(parts[2], m)


def _const_spec(shape):
    zeros = (0,) * len(shape)
    return pl.BlockSpec(shape, lambda *_: zeros, pipeline_mode=pl.Buffered(1))


def _params(n_axes):
    return pltpu.CompilerParams(dimension_semantics=("arbitrary",) * n_axes,
                                vmem_limit_bytes=VMEM_LIMIT)


def _log_sigmoid(z):
    return jnp.minimum(z, 0.0) - jnp.log1p(jnp.exp(-jnp.abs(z)))


def _gate_tables(g, triu, ones, tab_ref, row_ref):
    nc = g.shape[1] // CHUNK
    i_f = g[0:HEADS]
    lf_f = _log_sigmoid(g[HEADS:2 * HEADS])
    i_b = g[2 * HEADS:3 * HEADS]
    lf_b = _log_sigmoid(g[3 * HEADS:4 * HEADS])

    def chunk(a, c):
        return a[:, c * CHUNK:(c + 1) * CHUNK]

    stacked = jnp.concatenate([chunk(lf_f, c) for c in range(nc)]
                              + [chunk(lf_b, c) for c in range(nc)], axis=0)
    parts = _split3(stacked)
    cum = _dot3(parts, triu)
    tot = _dot3(parts, ones)
    pad = jnp.zeros((GATE_LANES - 2 * HEADS, CHUNK), F32)
    for c in range(nc):
        fwd = slice(c * HEADS, (c + 1) * HEADS)
        bwd = slice((nc + c) * HEADS, (nc + c + 1) * HEADS)
        b_f, tot_f = cum[fwd], tot[fwd]
        e_b, tot_b = cum[bwd] - chunk(lf_b, c), tot[bwd]
        a_f = chunk(i_f, c) - b_f
        a_b = chunk(i_b, c) + e_b
        groups = {R_AF: a_f * LOG2_E, R_AB: a_b * LOG2_E, R_TOTB: tot_b * LOG2_E,
                  R_WKF: jnp.exp(tot_f + a_f), R_WKB: jnp.exp(a_b),
                  R_DECF: jnp.exp(tot_f), R_DECB: jnp.exp(tot_b)}
        for r, val in groups.items():
            row_ref[0, c, r * HEADS:(r + 1) * HEADS, :] = val
        cols = jnp.concatenate([b_f * LOG2_E, -e_b * LOG2_E, pad], axis=0)
        tab_ref[c * CHUNK:(c + 1) * CHUNK, :] = cols.T


def _proj_kernel(*refs, n_cast):
    x_ref, g_ref, wt_ref, gb_ref, hg_ref, triu_ref, ones_ref = refs[:7]
    cast_in = refs[7:7 + n_cast]
    q_ref, kt_ref, v_ref, og_ref, tab_ref, row_ref = refs[7 + n_cast:13 + n_cast]
    cast_out = refs[13 + n_cast:]
    for src, dst in zip(cast_in, cast_out):
        dst[...] = src[...].astype(BF16)
    nqk = HEADS * DQK
    hd = HEADS * DV

    def w_rows(lo, hi):
        return wt_ref[lo:hi, :].astype(BF16)

    w_q = w_rows(0, nqk)
    w_kg = jnp.concatenate([w_rows(nqk, 2 * nqk),
                            w_rows(2 * nqk + 2 * hd, 2 * nqk + 2 * hd + N_GATES)], axis=0)
    w_v = w_rows(2 * nqk, 2 * nqk + hd)
    w_o = w_rows(2 * nqk + hd, 2 * nqk + 2 * hd)
    u = _rms(x_ref[...], g_ref[...]).astype(BF16)
    ktg = _dot_nt(w_kg, u)
    kt = ktg[0:nqk].astype(BF16)
    for c in range(kt_ref.shape[1]):
        kt_ref[0, c] = kt[:, c * CHUNK:(c + 1) * CHUNK]
    g = ktg[nqk:nqk + N_GATES] + gb_ref[...]
    g = GATE_SOFTCAP * jnp.tanh(g / GATE_SOFTCAP)
    _gate_tables(g, triu_ref[...], ones_ref[...], tab_ref, row_ref)
    o = _dot_nt(u, w_o)
    og_ref[...] = (jax.nn.sigmoid(o) * hg_ref[...]).astype(BF16)
    v_ref[...] = _dot_nt(u, w_v).astype(BF16)
    q_ref[...] = (_dot_nt(u, w_q) * (DQK ** -0.5)).astype(BF16)


def _row_slab_spec(shape, steps):
    rows = shape[-2]
    assert rows % steps == 0 and (rows // steps) % 16 == 0
    block = tuple(shape[:-2]) + (rows // steps, shape[-1])
    lead = (0,) * (len(shape) - 2)
    return pl.BlockSpec(block, lambda i: lead + (i, 0))


def _proj_call(x2, g, wt, gb, head_g, to_cast, tm, batch, seq):
    m, d = x2.shape
    hd = HEADS * DV
    nqk = HEADS * DQK
    assert tm % CHUNK == 0 and seq % tm == 0
    tiles_per_seq = seq // tm
    steps = m // tm
    row = lambda n: pl.BlockSpec((tm, n), lambda i: (i, 0))
    chunk_major = lambda n: pl.BlockSpec((1, tm // CHUNK, n, CHUNK),
                                         lambda i: (i // tiles_per_seq, i % tiles_per_seq, 0, 0))
    cast_specs = [_row_slab_spec(a.shape, steps) for a in to_cast]
    triu = jnp.asarray(np.triu(np.ones((CHUNK, CHUNK), np.float32)), BF16)
    ones = jnp.asarray(np.ones((CHUNK, CHUNK), np.float32), BF16)
    outs = pl.pallas_call(
        functools.partial(_proj_kernel, n_cast=len(to_cast)),
        grid=(steps,),
        in_specs=[row(d), _const_spec((1, d)), _const_spec(wt.shape),
                  _const_spec((N_GATES, 1)), _const_spec((1, hd)),
                  _const_spec((CHUNK, CHUNK)), _const_spec((CHUNK, CHUNK))] + cast_specs,
        out_specs=[row(nqk), chunk_major(nqk), row(hd), row(hd), row(GATE_LANES),
                   chunk_major(ROW_TABLE_ROWS)] + cast_specs,
        out_shape=[jax.ShapeDtypeStruct((m, nqk), BF16),
                   jax.ShapeDtypeStruct((batch, seq // CHUNK, nqk, CHUNK), BF16),
                   jax.ShapeDtypeStruct((m, hd), BF16),
                   jax.ShapeDtypeStruct((m, hd), BF16),
                   jax.ShapeDtypeStruct((m, GATE_LANES), F32),
                   jax.ShapeDtypeStruct((batch, seq // CHUNK, ROW_TABLE_ROWS, CHUNK), F32)]
                  + [jax.ShapeDtypeStruct(a.shape, BF16) for a in to_cast],
        compiler_params=_params(1),
        name="mlstm_proj",
    )(x2, g, wt, gb, head_g, triu, ones, *to_cast)
    return outs[:6], outs[6:]


def _mlstm_kernel(q_ref, kt_ref, v_ref, og_ref, tab_ref, row_ref, out_ref,
                  u_ref, sf_ref, sb_ref):
    nc = kt_ref.shape[1]
    ones_blk = jnp.ones((CHUNK, DV), BF16)
    zero_k = jnp.zeros((DQK, CHUNK), BF16)
    zero_state = jnp.zeros((DQK, 2 * DV), BF16)
    zero_q = jnp.zeros((HALF, 2 * DQK), BF16)
    t_idx = lax.broadcasted_iota(jnp.int32, (HALF, HALF), 0)
    s_idx = lax.broadcasted_iota(jnp.int32, (HALF, HALF), 1)
    lower = s_idx <= t_idx
    upper = s_idx >= t_idx

    def head_steps(h):
        pair_lanes = slice((h // 2) * 2 * DQK, (h // 2 + 1) * 2 * DQK)
        k_rows = slice(h * DQK, (h + 1) * DQK)
        v_lanes = slice(h * DV, (h + 1) * DV)

        def pad_rows(a, zero, h=h):
            return jnp.concatenate([a, zero] if h % 2 == 0 else [zero, a], axis=0)

        def chunk_rows(c):
            return pl.ds(pl.multiple_of(c * CHUNK, CHUNK), CHUNK)

        def vaug(rows, v_lanes=v_lanes):
            return jnp.concatenate([v_ref[rows, v_lanes], ones_blk], axis=1)

        def row(c, group, h=h):
            return row_ref[0, c, group * HEADS + h:group * HEADS + h + 1, :]

        def col(tab, group, h=h):
            return tab[:, group * HEADS + h:group * HEADS + h + 1]

        def contrib_step(c, carry):
            kt = kt_ref[0, c, k_rows, :].astype(F32)
            kw = jnp.concatenate([(kt * row(c, R_WKF)).astype(BF16),
                                  (kt * row(c, R_WKB)).astype(BF16)], axis=0)
            u_ref[c] = _dot(kw, vaug(chunk_rows(c)))
            return carry

        def chunk_scalar(c, group, h=h):
            return row_ref[0, c, group * HEADS + h:group * HEADS + h + 1, 0:1]

        def run_scans():
            state = jnp.zeros((DQK, 2 * DV), F32)
            for c in range(nc):
                sf_ref[c] = state.astype(BF16)
                state = chunk_scalar(c, R_DECF) * state + u_ref[c, 0:DQK, :]
            state = jnp.zeros((DQK, 2 * DV), F32)
            for c in reversed(range(nc)):
                sb_ref[c] = state.astype(BF16)
                state = chunk_scalar(c, R_DECB) * state + u_ref[c, DQK:2 * DQK, :]

        def out_step(c, carry):
            rows = chunk_rows(c)
            q = q_ref[rows, pair_lanes]
            s = _dot(q, pad_rows(kt_ref[0, c, k_rows, :], zero_k))
            tab = tab_ref[rows, :]
            c_f = jnp.broadcast_to(col(tab, G_CF), (CHUNK, HALF))
            c_b = jnp.broadcast_to(col(tab, G_CB), (CHUNK, HALF))
            a_f = row(c, R_AF)
            a_b = row(c, R_AB)

            def blk(t_half, s_half, c_col, a_row, mask=None):
                e = c_col[t_half] + a_row[:, s_half]
                if mask is not None:
                    e = jnp.where(mask, e, -jnp.inf)
                return (s[t_half, s_half] * jnp.exp2(e)).astype(BF16)

            qf32 = q.astype(F32)
            q_f = (qf32 * jnp.exp2(c_f[:, :2 * DQK])).astype(BF16)
            q_b = (qf32 * jnp.exp2(c_b[:, :2 * DQK] + chunk_scalar(c, R_TOTB))).astype(BF16)
            va = vaug(rows)
            rhs_lo = jnp.concatenate([va[LO], pad_rows(sf_ref[c], zero_state)], axis=0)
            rhs_hi = jnp.concatenate([va[HI], pad_rows(sb_ref[c], zero_state)], axis=0)
            zq = zero_q
            lhs_lo = jnp.concatenate(
                [jnp.concatenate([blk(LO, LO, c_f, a_f, lower), q_f[LO]], axis=1),
                 jnp.concatenate([blk(HI, LO, c_f, a_f), q_f[HI]], axis=1),
                 jnp.concatenate([blk(LO, LO, c_b, a_b, upper), zq], axis=1)], axis=0)
            lhs_hi = jnp.concatenate(
                [jnp.concatenate([blk(HI, HI, c_f, a_f, lower), zq], axis=1),
                 jnp.concatenate([blk(LO, HI, c_b, a_b), q_b[LO]], axis=1),
                 jnp.concatenate([blk(HI, HI, c_b, a_b, upper), q_b[HI]], axis=1)], axis=0)
            r_lo = _dot(lhs_lo, rhs_lo)
            r_hi = _dot(lhs_hi, rhs_hi)
            nd_f = jnp.concatenate([r_lo[0:HALF], r_lo[HALF:2 * HALF] + r_hi[0:HALF]], axis=0)
            nd_b = jnp.concatenate([r_lo[2 * HALF:] + r_hi[HALF:2 * HALF], r_hi[2 * HALF:]], axis=0)
            hh = (nd_f[:, :DV] / jnp.maximum(jnp.abs(nd_f[:, DV:]), 1.0)
                  + nd_b[:, :DV] / jnp.maximum(jnp.abs(nd_b[:, DV:]), 1.0))
            hn = hh * lax.rsqrt(jnp.mean(hh * hh, axis=-1, keepdims=True) + EPS)
            gated = hn * og_ref[rows, v_lanes].astype(F32)
            out_ref[rows, v_lanes] = gated.astype(BF16)
            return carry

        return contrib_step, run_scans, out_step

    steps = [head_steps(h) for h in range(HEADS)]
    lax.fori_loop(0, nc, steps[0][0], 0, unroll=MLSTM_UNROLL)
    steps[0][1]()
    for h in range(HEADS):
        out_step = steps[h][2]
        if h + 1 < HEADS:
            next_contrib = steps[h + 1][0]
            lax.fori_loop(0, nc, lambda c, carry: next_contrib(c, out_step(c, carry)), 0,
                          unroll=MLSTM_UNROLL)
            steps[h + 1][1]()
        else:
            lax.fori_loop(0, nc, out_step, 0, unroll=MLSTM_UNROLL)


def _mlstm_call(q, kt, v, og, tab, rowtab, batch, seq):
    hd = HEADS * DV
    nqk = HEADS * DQK
    nc = seq // CHUNK
    seq_blk = lambda n: pl.BlockSpec((seq, n), lambda b: (b, 0))
    return pl.pallas_call(
        _mlstm_kernel,
        grid=(batch,),
        in_specs=[seq_blk(nqk), pl.BlockSpec((1, nc, nqk, CHUNK), lambda b: (b, 0, 0, 0)),
                  seq_blk(hd), seq_blk(hd), seq_blk(GATE_LANES),
                  pl.BlockSpec((1, nc, ROW_TABLE_ROWS, CHUNK), lambda b: (b, 0, 0, 0))],
        out_specs=seq_blk(hd),
        out_shape=jax.ShapeDtypeStruct((batch * seq, hd), BF16),
        scratch_shapes=[pltpu.VMEM((nc, 2 * DQK, 2 * DV), F32),
                        pltpu.VMEM((nc, DQK, 2 * DV), BF16), pltpu.VMEM((nc, DQK, 2 * DV), BF16)],
        compiler_params=_params(1),
        name="mlstm_core",
    )(q, kt, v, og, tab, rowtab)


def _mlp_kernel(*refs, has_scale, final_norm, ff_chunk, layer):
    refs = list(refs)
    x_ref, a_ref, wo_ref = refs[:3]
    del refs[:3]
    sc_ref = refs.pop(0) if has_scale else None
    g_ref, w1_hbm, w2_hbm = refs[:3]
    del refs[:3]
    fg_ref = refs.pop(0) if final_norm else None
    out_ref, w1_ref, w2_ref, sem = refs
    n_chunks = w1_ref.shape[1] // ff_chunk

    def w1_copy(c):
        sl = slice(c * ff_chunk, (c + 1) * ff_chunk)
        return pltpu.make_async_copy(w1_hbm.at[layer, :, sl], w1_ref.at[:, sl], sem.at[0, c])

    def w2_copy(c):
        sl = slice(c * ff_chunk, (c + 1) * ff_chunk)
        return pltpu.make_async_copy(w2_hbm.at[layer, sl, :], w2_ref.at[sl, :], sem.at[1, c])

    first_step = pl.program_id(0) == 0

    @pl.when(first_step)
    def _():
        for c in range(n_chunks):
            w1_copy(c).start()
            w2_copy(c).start()

    r = _dot(a_ref[...], wo_ref[...])
    if has_scale:
        r = r * sc_ref[...]
    x1 = x_ref[...] + r
    u = _rms(x1, g_ref[...]).astype(BF16)

    @pl.when(first_step)
    def _():
        for c in range(n_chunks):
            w1_copy(c).wait()
            w2_copy(c).wait()

    y = x1
    for c in range(n_chunks):
        cols = slice(c * ff_chunk, (c + 1) * ff_chunk)
        hid = jnp.square(jnp.maximum(_dot(u, w1_ref[:, cols]), 0.0)).astype(BF16)
        y = y + _dot(hid, w2_ref[cols, :])
    if final_norm:
        y = _rms(y, fg_ref[...])
    out_ref[...] = y


def _mlp_call(x2, a, wo, scale, g, w1, w2, layer, final_g, tm, ff_chunk=1024):
    m, d = x2.shape
    dff = w1.shape[2]
    row = lambda n: pl.BlockSpec((tm, n), lambda i: (i, 0))
    hbm = pl.BlockSpec(memory_space=pl.ANY)
    args = [x2, a, wo]
    specs = [row(d), row(a.shape[1]), _const_spec(wo.shape)]
    if scale is not None:
        args.append(scale)
        specs.append(_const_spec((1, d)))
    args += [g, w1, w2]
    specs += [_const_spec((1, d)), hbm, hbm]
    if final_g is not None:
        args.append(final_g)
        specs.append(_const_spec((1, d)))
    body = functools.partial(_mlp_kernel, has_scale=scale is not None,
                             final_norm=final_g is not None, ff_chunk=ff_chunk, layer=layer)
    return pl.pallas_call(
        body,
        grid=(m // tm,),
        in_specs=specs,
        out_specs=row(d),
        out_shape=jax.ShapeDtypeStruct((m, d), F32),
        scratch_shapes=[pltpu.VMEM((d, dff), BF16), pltpu.VMEM((dff, d), BF16),
                        pltpu.SemaphoreType.DMA((2, dff // ff_chunk))],
        compiler_params=_params(1),
        name="mlp",
    )(*args)


def _pool_kernel(x_ref, g_ref, win_ref, wgrp_ref, out_ref, a_scr, *, proj_rows, pool_rows):
    seq, d = x_ref.shape
    gdim = d // len(POOL_WINDOWS)
    a_scr[0:POOL_HALO, :] = jnp.zeros((POOL_HALO, d), F32)
    a_scr[POOL_HALO + seq:, :] = jnp.zeros((POOL_HALO, d), F32)
    for t in range(seq // proj_rows):
        rows = slice(t * proj_rows, (t + 1) * proj_rows)
        u = _rms(x_ref[rows, :], g_ref[...]).astype(BF16)
        a_scr[POOL_HALO + t * proj_rows:POOL_HALO + (t + 1) * proj_rows, :] = _dot(u, win_ref[...])

    n_tiles = seq // pool_rows
    n = pool_rows + 2 * POOL_HALO
    centre = slice(POOL_HALO, POOL_HALO + pool_rows)
    for t in range(n_tiles):
        t0 = t * pool_rows
        for gi, w in enumerate(POOL_WINDOWS):
            cols = slice(gi * gdim, (gi + 1) * gdim)
            xw = a_scr[t0:t0 + n, cols]
            fwd = xw
            span = 1
            while 2 * span < w:
                fwd = fwd + pltpu.roll(fwd, n - span, axis=0)
                span *= 2
            s = fwd + pltpu.roll(fwd, w // 2, axis=0)
            if 0 < t < n_tiles - 1:
                mean = s[centre, :] * (1.0 / w)
            else:
                pos = t0 + lax.broadcasted_iota(jnp.int32, (pool_rows, gdim), 0)
                count = jnp.minimum(pos + w // 2, seq) - jnp.maximum(pos - w // 2, 0)
                mean = s[centre, :] / count.astype(F32)
            pooled = mean - xw[centre, :]
            mixed = _dot(pooled.astype(BF16), wgrp_ref[gi])
            out_ref[t0:t0 + pool_rows, cols] = mixed.astype(BF16)


def _pool_call(x2, g, w_in, w_group, batch, seq, proj_rows=512, pool_rows=256):
    d = x2.shape[1]
    seq_blk = pl.BlockSpec((seq, d), lambda b: (b, 0))
    body = functools.partial(_pool_kernel, proj_rows=proj_rows, pool_rows=pool_rows)
    return pl.pallas_call(
        body,
        grid=(batch,),
        in_specs=[seq_blk, _const_spec((1, d)), _const_spec(w_in.shape), _const_spec(w_group.shape)],
        out_specs=seq_blk,
        out_shape=jax.ShapeDtypeStruct((batch * seq, d), BF16),
        scratch_shapes=[pltpu.VMEM((seq + 2 * POOL_HALO, d), F32)],
        compiler_params=_params(1),
        name="pool_mixer",
    )(x2, g, w_in, w_group)


def kernel(x, mix_norm_g, mlp_norm_g, mlstm_w_in, mlstm_gate_b, mlstm_head_g, mlstm_w_out,
           pool_w_in, pool_w_group, pool_w_out, pool_scale, mlp_w1, mlp_w2, final_norm_g):
    batch, seq, d = x.shape
    assert seq % CHUNK == 0 and d == HEADS * DV and HALF == 2 * DQK
    m = batch * seq
    tm = 1024
    x2 = x.reshape(m, d)
    vec = lambda a: a.reshape(1, -1).astype(F32)

    w_in_t = mlstm_w_in[0].T
    gb = mlstm_gate_b[0].astype(F32).reshape(N_GATES, 1)
    n_groups, gdim = pool_w_group.shape[1], pool_w_group.shape[2]

    later_f32 = [mlp_w1, mlp_w2, mlstm_w_out[0], pool_w_in[0], pool_w_out[0],
                 pool_w_group[0].reshape(n_groups * gdim, gdim)]
    (q, kt, v, og, tab, rowtab), (w1, w2, w_out, p_in, p_out, p_group) = _proj_call(
        x2, vec(mix_norm_g[0]), w_in_t, gb, vec(mlstm_head_g[0]), later_f32, tm, batch, seq)
    hmix = _mlstm_call(q, kt, v, og, tab, rowtab, batch, seq)
    x2 = _mlp_call(x2, hmix, w_out, None, vec(mlp_norm_g[0]), w1, w2, 0, None, tm)

    mixed = _pool_call(x2, vec(mix_norm_g[1]), p_in, p_group.reshape(n_groups, gdim, gdim),
                       batch, seq)
    x2 = _mlp_call(x2, mixed, p_out, vec(pool_scale[0]), vec(mlp_norm_g[1]),
                   w1, w2, 1, vec(final_norm_g), tm)
    return x2.reshape(batch, seq, d)
```

```python
import functools

import numpy as np
import jax
import jax.numpy as jnp
from jax import lax
from jax.experimental import pallas as pl
from jax.experimental.pallas import tpu as pltpu

F32 = jnp.float32
BF16 = jnp.bfloat16

HEADS = 8
DQK = 64
DV = 128
CHUNK = 256
HALF = CHUNK // 2
LO, HI = slice(0, HALF), slice(HALF, CHUNK)
MLSTM_UNROLL = 8
GATE_SOFTCAP = 15.0
POOL_WINDOWS = (2, 4, 8, 16)
POOL_HALO = 8
EPS = 1e-6
LOG2_E = 1.4426950408889634
GATE_LANES = 128
VMEM_LIMIT = 58 * 1024 * 1024

N_GATES = 4 * HEADS
G_CF, G_CB = range(2)
R_AF, R_AB, R_TOTB, R_WKF, R_WKB, R_DECF, R_DECB = range(7)
ROW_TABLE_ROWS = 7 * HEADS


def _rms(x, g):
    ms = jnp.mean(x * x, axis=-1, keepdims=True)
    return x * lax.rsqrt(ms + EPS) * g


def _dot(a, b):
    return jnp.dot(a, b, preferred_element_type=F32)


def _dot_nt(a, b):
    return lax.dot_general(a, b, (((1,), (1,)), ((), ())), preferred_element_type=F32)


def _split3(x):
    hi = x.astype(BF16)
    r1 = x - hi.astype(F32)
    mid = r1.astype(BF16)
    lo = (r1 - mid.astype(F32)).astype(BF16)
    return hi, mid, lo


def _dot3(parts, m):
    return _dot(parts[0], m) + _dot(parts[1], m) + _dot(parts[2], m)


def _const_spec(shape):
    zeros = (0,) * len(shape)
    return pl.BlockSpec(shape, lambda *_: zeros, pipeline_mode=pl.Buffered(1))


def _params(n_axes):
    return pltpu.CompilerParams(dimension_semantics=("arbitrary",) * n_axes,
                                vmem_limit_bytes=VMEM_LIMIT)


def _log_sigmoid(z):
    return jnp.minimum(z, 0.0) - jnp.log1p(jnp.exp(-jnp.abs(z)))


def _gate_tables(g, triu, ones, tab_ref, row_ref):
    nc = g.shape[1] // CHUNK
    i_f = g[0:HEADS]
    lf_f = _log_sigmoid(g[HEADS:2 * HEADS])
    i_b = g[2 * HEADS:3 * HEADS]
    lf_b = _log_sigmoid(g[3 * HEADS:4 * HEADS])

    def chunk(a, c):
        return a[:, c * CHUNK:(c + 1) * CHUNK]

    stacked = jnp.concatenate([chunk(lf_f, c) for c in range(nc)]
                              + [chunk(lf_b, c) for c in range(nc)], axis=0)
    parts = _split3(stacked)
    cum = _dot3(parts, triu)
    tot = _dot3(parts, ones)
    pad = jnp.zeros((GATE_LANES - 2 * HEADS, CHUNK), F32)
    for c in range(nc):
        fwd = slice(c * HEADS, (c + 1) * HEADS)
        bwd = slice((nc + c) * HEADS, (nc + c + 1) * HEADS)
        b_f, tot_f = cum[fwd], tot[fwd]
        e_b, tot_b = cum[bwd] - chunk(lf_b, c), tot[bwd]
        a_f = chunk(i_f, c) - b_f
        a_b = chunk(i_b, c) + e_b
        groups = {R_AF: a_f * LOG2_E, R_AB: a_b * LOG2_E, R_TOTB: tot_b * LOG2_E,
                  R_WKF: jnp.exp(tot_f + a_f), R_WKB: jnp.exp(a_b),
                  R_DECF: jnp.exp(tot_f), R_DECB: jnp.exp(tot_b)}
        for r, val in groups.items():
            row_ref[0, c, r * HEADS:(r + 1) * HEADS, :] = val
        cols = jnp.concatenate([b_f * LOG2_E, -e_b * LOG2_E, pad], axis=0)
        tab_ref[c * CHUNK:(c + 1) * CHUNK, :] = cols.T


def _proj_kernel(*refs, n_cast):
    x_ref, g_ref, wt_ref, gb_ref, hg_ref, triu_ref, ones_ref = refs[:7]
    cast_in = refs[7:7 + n_cast]
    q_ref, kt_ref, v_ref, og_ref, tab_ref, row_ref, kw_ref = refs[7 + n_cast:14 + n_cast]
    cast_out = refs[14 + n_cast:]
    for src, dst in zip(cast_in, cast_out):
        dst[...] = src[...].astype(BF16)
    nqk = HEADS * DQK
    hd = HEADS * DV

    def w_rows(lo, hi):
        return wt_ref[lo:hi, :].astype(BF16)

    w_q = w_rows(0, nqk)
    w_kg = jnp.concatenate([w_rows(nqk, 2 * nqk),
                            w_rows(2 * nqk + 2 * hd, 2 * nqk + 2 * hd + N_GATES)], axis=0)
    w_v = w_rows(2 * nqk, 2 * nqk + hd)
    w_o = w_rows(2 * nqk + hd, 2 * nqk + 2 * hd)
    u = _rms(x_ref[...], g_ref[...]).astype(BF16)
    ktg = _dot_nt(w_kg, u)
    kt = ktg[0:nqk].astype(BF16)
    for c in range(kt_ref.shape[1]):
        kt_ref[0, c] = kt[:, c * CHUNK:(c + 1) * CHUNK]
    g = ktg[nqk:nqk + N_GATES] + gb_ref[...]
    g = GATE_SOFTCAP * jnp.tanh(g / GATE_SOFTCAP)
    _gate_tables(g, triu_ref[...], ones_ref[...], tab_ref, row_ref)
    for c in range(kt_ref.shape[1]):
        for h in range(HEADS):
            kh = ktg[h * DQK:(h + 1) * DQK, c * CHUNK:(c + 1) * CHUNK]
            for d, group in enumerate((R_WKF, R_WKB)):
                wk = row_ref[0, c, group * HEADS + h:group * HEADS + h + 1, :]
                kw_ref[0, c, (2 * h + d) * DQK:(2 * h + d + 1) * DQK, :] = (kh * wk).astype(BF16)
    o = _dot_nt(u, w_o)
    og_ref[...] = (jax.nn.sigmoid(o) * hg_ref[...]).astype(BF16)
    v_ref[...] = _dot_nt(u, w_v).astype(BF16)
    q_ref[...] = (_dot_nt(u, w_q) * (DQK ** -0.5)).astype(BF16)


def _row_slab_spec(shape, steps):
    rows = shape[-2]
    assert rows % steps == 0 and (rows // steps) % 16 == 0
    block = tuple(shape[:-2]) + (rows // steps, shape[-1])
    lead = (0,) * (len(shape) - 2)
    return pl.BlockSpec(block, lambda i: lead + (i, 0))


def _proj_call(x2, g, wt, gb, head_g, to_cast, tm, batch, seq):
    m, d = x2.shape
    hd = HEADS * DV
    nqk = HEADS * DQK
    assert tm % CHUNK == 0 and seq % tm == 0
    tiles_per_seq = seq // tm
    steps = m // tm
    row = lambda n: pl.BlockSpec((tm, n), lambda i: (i, 0))
    chunk_major = lambda n: pl.BlockSpec((1, tm // CHUNK, n, CHUNK),
                                         lambda i: (i // tiles_per_seq, i % tiles_per_seq, 0, 0))
    cast_specs = [_row_slab_spec(a.shape, steps) for a in to_cast]
    triu = jnp.asarray(np.triu(np.ones((CHUNK, CHUNK), np.float32)), BF16)
    ones = jnp.asarray(np.ones((CHUNK, CHUNK), np.float32), BF16)
    outs = pl.pallas_call(
        functools.partial(_proj_kernel, n_cast=len(to_cast)),
        grid=(steps,),
        in_specs=[row(d), _const_spec((1, d)), _const_spec(wt.shape),
                  _const_spec((N_GATES, 1)), _const_spec((1, hd)),
                  _const_spec((CHUNK, CHUNK)), _const_spec((CHUNK, CHUNK))] + cast_specs,
        out_specs=[row(nqk), chunk_major(nqk), row(hd), row(hd), row(GATE_LANES),
                   chunk_major(ROW_TABLE_ROWS), chunk_major(2 * nqk)] + cast_specs,
        out_shape=[jax.ShapeDtypeStruct((m, nqk), BF16),
                   jax.ShapeDtypeStruct((batch, seq // CHUNK, nqk, CHUNK), BF16),
                   jax.ShapeDtypeStruct((m, hd), BF16),
                   jax.ShapeDtypeStruct((m, hd), BF16),
                   jax.ShapeDtypeStruct((m, GATE_LANES), F32),
                   jax.ShapeDtypeStruct((batch, seq // CHUNK, ROW_TABLE_ROWS, CHUNK), F32),
                   jax.ShapeDtypeStruct((batch, seq // CHUNK, 2 * nqk, CHUNK), BF16)]
                  + [jax.ShapeDtypeStruct(a.shape, BF16) for a in to_cast],
        compiler_params=_params(1),
        name="mlstm_proj",
    )(x2, g, wt, gb, head_g, triu, ones, *to_cast)
    return outs[:7], outs[7:]


def _mlstm_kernel(q_ref, kt_ref, kw_ref, v_ref, og_ref, tab_ref, row_ref, out_ref,
                  u_ref, sf_ref, sb_ref):
    nc = kt_ref.shape[1]
    ones_blk = jnp.ones((CHUNK, DV), BF16)
    zero_k = jnp.zeros((DQK, CHUNK), BF16)
    zero_state = jnp.zeros((DQK, 2 * DV), BF16)
    zero_q = jnp.zeros((HALF, 2 * DQK), BF16)
    t_idx = lax.broadcasted_iota(jnp.int32, (HALF, HALF), 0)
    s_idx = lax.broadcasted_iota(jnp.int32, (HALF, HALF), 1)
    lower = s_idx <= t_idx
    upper = s_idx >= t_idx

    def head_steps(h):
        pair_lanes = slice((h // 2) * 2 * DQK, (h // 2 + 1) * 2 * DQK)
        k_rows = slice(h * DQK, (h + 1) * DQK)
        v_lanes = slice(h * DV, (h + 1) * DV)

        def pad_rows(a, zero, h=h):
            return jnp.concatenate([a, zero] if h % 2 == 0 else [zero, a], axis=0)

        def chunk_rows(c):
            return pl.ds(pl.multiple_of(c * CHUNK, CHUNK), CHUNK)

        def vaug(rows, v_lanes=v_lanes):
            return jnp.concatenate([v_ref[rows, v_lanes], ones_blk], axis=1)

        def row(c, group, h=h):
            return row_ref[0, c, group * HEADS + h:group * HEADS + h + 1, :]

        def col(tab, group, h=h):
            return tab[:, group * HEADS + h:group * HEADS + h + 1]

        def contrib_step(c, carry):
            kw = kw_ref[0, c, h * 2 * DQK:(h + 1) * 2 * DQK, :]
            u_ref[c] = _dot(kw, vaug(chunk_rows(c)))
            return carry

        def chunk_scalar(c, group, h=h):
            return row_ref[0, c, group * HEADS + h:group * HEADS + h + 1, 0:1]

        def run_scans():
            state = jnp.zeros((DQK, 2 * DV), F32)
            for c in range(nc):
                sf_ref[c] = state.astype(BF16)
                state = chunk_scalar(c, R_DECF) * state + u_ref[c, 0:DQK, :]
            state = jnp.zeros((DQK, 2 * DV), F32)
            for c in reversed(range(nc)):
                sb_ref[c] = state.astype(BF16)
                state = chunk_scalar(c, R_DECB) * state + u_ref[c, DQK:2 * DQK, :]

        def out_step(c, carry):
            rows = chunk_rows(c)
            q = q_ref[rows, pair_lanes]
            s = _dot(q, pad_rows(kt_ref[0, c, k_rows, :], zero_k))
            tab = tab_ref[rows, :]
            c_f = jnp.broadcast_to(col(tab, G_CF), (CHUNK, HALF))
            c_b = jnp.broadcast_to(col(tab, G_CB), (CHUNK, HALF))
            a_f = row(c, R_AF)
            a_b = row(c, R_AB)

            def blk(t_half, s_half, c_col, a_row, mask=None):
                e = c_col[t_half] + a_row[:, s_half]
                if mask is not None:
                    e = jnp.where(mask, e, -jnp.inf)
                return (s[t_half, s_half] * jnp.exp2(e)).astype(BF16)

            qf32 = q.astype(F32)
            q_f = (qf32 * jnp.exp2(c_f[:, :2 * DQK])).astype(BF16)
            q_b = (qf32 * jnp.exp2(c_b[:, :2 * DQK] + chunk_scalar(c, R_TOTB))).astype(BF16)
            va = vaug(rows)
            rhs_lo = jnp.concatenate([va[LO], pad_rows(sf_ref[c], zero_state)], axis=0)
            rhs_hi = jnp.concatenate([va[HI], pad_rows(sb_ref[c], zero_state)], axis=0)
            zq = zero_q
            lhs_lo = jnp.concatenate(
                [jnp.concatenate([blk(LO, LO, c_f, a_f, lower), q_f[LO]], axis=1),
                 jnp.concatenate([blk(HI, LO, c_f, a_f), q_f[HI]], axis=1),
                 jnp.concatenate([blk(LO, LO, c_b, a_b, upper), zq], axis=1)], axis=0)
            lhs_hi = jnp.concatenate(
                [jnp.concatenate([blk(HI, HI, c_f, a_f, lower), zq], axis=1),
                 jnp.concatenate([blk(LO, HI, c_b, a_b), q_b[LO]], axis=1),
                 jnp.concatenate([blk(HI, HI, c_b, a_b, upper), q_b[HI]], axis=1)], axis=0)
            r_lo = _dot(lhs_lo, rhs_lo)
            r_hi = _dot(lhs_hi, rhs_hi)
            nd_f = jnp.concatenate([r_lo[0:HALF], r_lo[HALF:2 * HALF] + r_hi[0:HALF]], axis=0)
            nd_b = jnp.concatenate([r_lo[2 * HALF:] + r_hi[HALF:2 * HALF], r_hi[2 * HALF:]], axis=0)
            hh = (nd_f[:, :DV] / jnp.maximum(jnp.abs(nd_f[:, DV:]), 1.0)
                  + nd_b[:, :DV] / jnp.maximum(jnp.abs(nd_b[:, DV:]), 1.0))
            hn = hh * lax.rsqrt(jnp.mean(hh * hh, axis=-1, keepdims=True) + EPS)
            gated = hn * og_ref[rows, v_lanes].astype(F32)
            out_ref[rows, v_lanes] = gated.astype(BF16)
            return carry

        return contrib_step, run_scans, out_step

    steps = [head_steps(h) for h in range(HEADS)]
    lax.fori_loop(0, nc, steps[0][0], 0, unroll=MLSTM_UNROLL)
    steps[0][1]()
    for h in range(HEADS):
        out_step = steps[h][2]
        if h + 1 < HEADS:
            next_contrib = steps[h + 1][0]
            lax.fori_loop(0, nc, lambda c, carry: next_contrib(c, out_step(c, carry)), 0,
                          unroll=MLSTM_UNROLL)
            steps[h + 1][1]()
        else:
            lax.fori_loop(0, nc, out_step, 0, unroll=MLSTM_UNROLL)


def _mlstm_call(q, kt, kw, v, og, tab, rowtab, batch, seq):
    hd = HEADS * DV
    nqk = HEADS * DQK
    nc = seq // CHUNK
    seq_blk = lambda n: pl.BlockSpec((seq, n), lambda b: (b, 0))
    return pl.pallas_call(
        _mlstm_kernel,
        grid=(batch,),
        in_specs=[seq_blk(nqk), pl.BlockSpec((1, nc, nqk, CHUNK), lambda b: (b, 0, 0, 0)),
                  pl.BlockSpec((1, nc, 2 * nqk, CHUNK), lambda b: (b, 0, 0, 0)),
                  seq_blk(hd), seq_blk(hd), seq_blk(GATE_LANES),
                  pl.BlockSpec((1, nc, ROW_TABLE_ROWS, CHUNK), lambda b: (b, 0, 0, 0))],
        out_specs=seq_blk(hd),
        out_shape=jax.ShapeDtypeStruct((batch * seq, hd), BF16),
        scratch_shapes=[pltpu.VMEM((nc, 2 * DQK, 2 * DV), F32),
                        pltpu.VMEM((nc, DQK, 2 * DV), BF16), pltpu.VMEM((nc, DQK, 2 * DV), BF16)],
        compiler_params=_params(1),
        name="mlstm_core",
    )(q, kt, kw, v, og, tab, rowtab)


def _mlp_kernel(*refs, has_scale, out_norm, ff_chunk):
    refs = list(refs)
    x_ref, a_ref, wo_ref = refs[:3]
    del refs[:3]
    sc_ref = refs.pop(0) if has_scale else None
    g_ref, w1_ref, w2_ref = refs[:3]
    del refs[:3]
    fg_ref = refs.pop(0) if out_norm else None
    out_ref = refs.pop(0)

    r = _dot(a_ref[...], wo_ref[...])
    if has_scale:
        r = r * sc_ref[...]
    x1 = x_ref[...] + r
    u = _rms(x1, g_ref[...]).astype(BF16)
    y = x1
    for c in range(w1_ref.shape[1] // ff_chunk):
        cols = slice(c * ff_chunk, (c + 1) * ff_chunk)
        hid = jnp.square(jnp.maximum(_dot(u, w1_ref[:, cols]), 0.0)).astype(BF16)
        y = y + _dot(hid, w2_ref[cols, :])
    if out_norm == "final":
        y = _rms(y, fg_ref[...])
    elif out_norm == "next":
        (u_next_ref,) = refs
        u_next_ref[...] = _rms(y, fg_ref[...]).astype(BF16)
    out_ref[...] = y


def _layer_spec(shape, layer):
    zeros = (0,) * (len(shape) - 1)
    return pl.BlockSpec((None,) + tuple(shape[1:]), lambda *_: (layer,) + zeros,
                        pipeline_mode=pl.Buffered(1))


def _mlp_call(x2, a, wo, scale, g, w1, w2, layer, out_g, out_norm, tm, ff_chunk=1024):
    m, d = x2.shape
    row = lambda n: pl.BlockSpec((tm, n), lambda i: (i, 0))
    args = [x2, a, wo]
    specs = [row(d), row(a.shape[1]), _const_spec(wo.shape)]
    if scale is not None:
        args.append(scale)
        specs.append(_const_spec((1, d)))
    args += [g, w1, w2]
    specs += [_const_spec((1, d)), _layer_spec(w1.shape, layer), _layer_spec(w2.shape, layer)]
    assert out_norm in ("final", "next")
    args.append(out_g)
    specs.append(_const_spec((1, d)))
    out_specs, out_shape = [row(d)], [jax.ShapeDtypeStruct((m, d), F32)]
    if out_norm == "next":
        out_specs.append(row(d))
        out_shape.append(jax.ShapeDtypeStruct((m, d), BF16))
    body = functools.partial(_mlp_kernel, has_scale=scale is not None, out_norm=out_norm,
                             ff_chunk=ff_chunk)
    outs = pl.pallas_call(
        body,
        grid=(m // tm,),
        in_specs=specs,
        out_specs=out_specs,
        out_shape=out_shape,
        compiler_params=_params(1),
        name="mlp",
    )(*args)
    return outs[0] if out_norm == "final" else tuple(outs)


def _pool_kernel(u_ref, win_ref, wgrp_ref, out_ref, a_scr, *, proj_rows, pool_rows):
    seq, d = u_ref.shape
    gdim = d // len(POOL_WINDOWS)
    a_scr[0:POOL_HALO, :] = jnp.zeros((POOL_HALO, d), F32)
    a_scr[POOL_HALO + seq:, :] = jnp.zeros((POOL_HALO, d), F32)
    for t in range(seq // proj_rows):
        rows = slice(t * proj_rows, (t + 1) * proj_rows)
        a_scr[POOL_HALO + t * proj_rows:POOL_HALO + (t + 1) * proj_rows, :] = _dot(u_ref[rows, :],
                                                                                   win_ref[...])

    n_tiles = seq // pool_rows
    n = pool_rows + 2 * POOL_HALO
    centre = slice(POOL_HALO, POOL_HALO + pool_rows)
    for t in range(n_tiles):
        t0 = t * pool_rows
        for gi, w in enumerate(POOL_WINDOWS):
            cols = slice(gi * gdim, (gi + 1) * gdim)
            xw = a_scr[t0:t0 + n, cols]
            fwd = xw
            span = 1
            while 2 * span < w:
                fwd = fwd + pltpu.roll(fwd, n - span, axis=0)
                span *= 2
            s = fwd + pltpu.roll(fwd, w // 2, axis=0)
            if 0 < t < n_tiles - 1:
                mean = s[centre, :] * (1.0 / w)
            else:
                pos = t0 + lax.broadcasted_iota(jnp.int32, (pool_rows, gdim), 0)
                count = jnp.minimum(pos + w // 2, seq) - jnp.maximum(pos - w // 2, 0)
                mean = s[centre, :] / count.astype(F32)
            pooled = mean - xw[centre, :]
            mixed = _dot(pooled.astype(BF16), wgrp_ref[gi])
            out_ref[t0:t0 + pool_rows, cols] = mixed.astype(BF16)


def _pool_call(u, w_in, w_group, batch, seq, proj_rows=512, pool_rows=256):
    d = u.shape[1]
    seq_blk = pl.BlockSpec((seq, d), lambda b: (b, 0))
    body = functools.partial(_pool_kernel, proj_rows=proj_rows, pool_rows=pool_rows)
    return pl.pallas_call(
        body,
        grid=(batch,),
        in_specs=[seq_blk, _const_spec(w_in.shape), _const_spec(w_group.shape)],
        out_specs=seq_blk,
        out_shape=jax.ShapeDtypeStruct((batch * seq, d), BF16),
        scratch_shapes=[pltpu.VMEM((seq + 2 * POOL_HALO, d), F32)],
        compiler_params=_params(1),
        name="pool_mixer",
    )(u, w_in, w_group)


def kernel(x, mix_norm_g, mlp_norm_g, mlstm_w_in, mlstm_gate_b, mlstm_head_g, mlstm_w_out,
           pool_w_in, pool_w_group, pool_w_out, pool_scale, mlp_w1, mlp_w2, final_norm_g):
    batch, seq, d = x.shape
    assert seq % CHUNK == 0 and d == HEADS * DV and HALF == 2 * DQK
    m = batch * seq
    tm = 1024
    x2 = x.reshape(m, d)
    vec = lambda a: a.reshape(1, -1).astype(F32)

    w_in_t = mlstm_w_in[0].T
    gb = mlstm_gate_b[0].astype(F32).reshape(N_GATES, 1)
    n_groups, gdim = pool_w_group.shape[1], pool_w_group.shape[2]

    later_f32 = [mlp_w1, mlp_w2, mlstm_w_out[0], pool_w_in[0], pool_w_out[0],
                 pool_w_group[0].reshape(n_groups * gdim, gdim)]
    (q, kt, v, og, tab, rowtab, kw), (w1, w2, w_out, p_in, p_out, p_group) = _proj_call(
        x2, vec(mix_norm_g[0]), w_in_t, gb, vec(mlstm_head_g[0]), later_f32, tm, batch, seq)
    hmix = _mlstm_call(q, kt, kw, v, og, tab, rowtab, batch, seq)
    x2, u_pool = _mlp_call(x2, hmix, w_out, None, vec(mlp_norm_g[0]), w1, w2, 0,
                           vec(mix_norm_g[1]), "next", tm)

    mixed = _pool_call(u_pool, p_in, p_group.reshape(n_groups, gdim, gdim), batch, seq)
    x2 = _mlp_call(x2, mixed, p_out, vec(pool_scale[0]), vec(mlp_norm_g[1]),
                   w1, w2, 1, vec(final_norm_g), "final", tm)
    return x2.reshape(batch, seq, d)
```

```python
import functools

import numpy as np
import jax
import jax.numpy as jnp
from jax import lax
from jax.experimental import pallas as pl
from jax.experimental.pallas import tpu as pltpu

F32 = jnp.float32
BF16 = jnp.bfloat16

HEADS = 8
DQK = 64
DV = 128
CHUNK = 256
HALF = CHUNK // 2
LO, HI = slice(0, HALF), slice(HALF, CHUNK)
MLSTM_UNROLL = 8
GATE_SOFTCAP = 15.0
POOL_WINDOWS = (2, 4, 8, 16)
POOL_HALO = 8
EPS = 1e-6
LOG2_E = 1.4426950408889634
GATE_LANES = 128
VMEM_LIMIT = 56 * 1024 * 1024

N_GATES = 4 * HEADS
G_CF, G_CB = range(2)
R_AF, R_AB, R_TOTB, R_WKF, R_WKB, R_DECF, R_DECB = range(7)
ROW_TABLE_ROWS = 7 * HEADS


def _rms(x, g):
    ms = jnp.mean(x * x, axis=-1, keepdims=True)
    return x * lax.rsqrt(ms + EPS) * g


def _dot(a, b):
    return jnp.dot(a, b, preferred_element_type=F32)


def _dot_nt(a, b):
    return lax.dot_general(a, b, (((1,), (1,)), ((), ())), preferred_element_type=F32)


def _split3(x):
    hi = x.astype(BF16)
    r1 = x - hi.astype(F32)
    mid = r1.astype(BF16)
    lo = (r1 - mid.astype(F32)).astype(BF16)
    return hi, mid, lo


def _dot3(parts, m):
    return _dot(parts[0], m) + _dot(parts[1], m) + _dot(parts[2], m)


def _const_spec(shape):
    zeros = (0,) * len(shape)
    return pl.BlockSpec(shape, lambda *_: zeros, pipeline_mode=pl.Buffered(1))


def _params(n_axes):
    return pltpu.CompilerParams(dimension_semantics=("arbitrary",) * n_axes,
                                vmem_limit_bytes=VMEM_LIMIT)


def _log_sigmoid(z):
    return jnp.minimum(z, 0.0) - jnp.log1p(jnp.exp(-jnp.abs(z)))


def _gate_tables(g, triu, ones, tab_ref, row_ref):
    nc = g.shape[1] // CHUNK
    i_f = g[0:HEADS]
    lf_f = _log_sigmoid(g[HEADS:2 * HEADS])
    i_b = g[2 * HEADS:3 * HEADS]
    lf_b = _log_sigmoid(g[3 * HEADS:4 * HEADS])

    def chunk(a, c):
        return a[:, c * CHUNK:(c + 1) * CHUNK]

    stacked = jnp.concatenate([chunk(lf_f, c) for c in range(nc)]
                              + [chunk(lf_b, c) for c in range(nc)], axis=0)
    parts = _split3(stacked)
    cum = _dot3(parts, triu)
    tot = _dot3(parts, ones)
    pad = jnp.zeros((GATE_LANES - 2 * HEADS, CHUNK), F32)
    for c in range(nc):
        fwd = slice(c * HEADS, (c + 1) * HEADS)
        bwd = slice((nc + c) * HEADS, (nc + c + 1) * HEADS)
        b_f, tot_f = cum[fwd], tot[fwd]
        e_b, tot_b = cum[bwd] - chunk(lf_b, c), tot[bwd]
        a_f = chunk(i_f, c) - b_f
        a_b = chunk(i_b, c) + e_b
        groups = {R_AF: a_f * LOG2_E, R_AB: a_b * LOG2_E, R_TOTB: tot_b * LOG2_E,
                  R_WKF: jnp.exp(tot_f + a_f), R_WKB: jnp.exp(a_b),
                  R_DECF: jnp.exp(tot_f), R_DECB: jnp.exp(tot_b)}
        for r, val in groups.items():
            row_ref[0, c, r * HEADS:(r + 1) * HEADS, :] = val
        cols = jnp.concatenate([b_f * LOG2_E, -e_b * LOG2_E, pad], axis=0)
        tab_ref[c * CHUNK:(c + 1) * CHUNK, :] = cols.T


def _proj_kernel(*refs, n_cast):
    x_ref, g_ref, wt_ref, gb_ref, hg_ref, triu_ref, ones_ref = refs[:7]
    cast_in = refs[7:7 + n_cast]
    q_ref, kt_ref, v_ref, og_ref, tab_ref, row_ref = refs[7 + n_cast:13 + n_cast]
    cast_out = refs[13 + n_cast:]
    for src, dst in zip(cast_in, cast_out):
        dst[...] = src[...].astype(BF16)
    nqk = HEADS * DQK
    hd = HEADS * DV

    def w_rows(lo, hi):
        return wt_ref[lo:hi, :].astype(BF16)

    w_q = w_rows(0, nqk)
    w_kg = jnp.concatenate([w_rows(nqk, 2 * nqk),
                            w_rows(2 * nqk + 2 * hd, 2 * nqk + 2 * hd + N_GATES)], axis=0)
    w_v = w_rows(2 * nqk, 2 * nqk + hd)
    w_o = w_rows(2 * nqk + hd, 2 * nqk + 2 * hd)
    u = _rms(x_ref[...], g_ref[...]).astype(BF16)
    ktg = _dot_nt(w_kg, u)
    kt = ktg[0:nqk].astype(BF16)
    for c in range(kt_ref.shape[1]):
        kt_ref[0, c] = kt[:, c * CHUNK:(c + 1) * CHUNK]
    g = ktg[nqk:nqk + N_GATES] + gb_ref[...]
    g = GATE_SOFTCAP * jnp.tanh(g / GATE_SOFTCAP)
    _gate_tables(g, triu_ref[...], ones_ref[...], tab_ref, row_ref)
    o = _dot_nt(u, w_o)
    og_ref[...] = (jax.nn.sigmoid(o) * hg_ref[...]).astype(BF16)
    v_ref[...] = _dot_nt(u, w_v).astype(BF16)
    q_ref[...] = (_dot_nt(u, w_q) * (DQK ** -0.5)).astype(BF16)


def _row_slab_spec(shape, steps):
    rows = shape[-2]
    assert rows % steps == 0 and (rows // steps) % 16 == 0
    block = tuple(shape[:-2]) + (rows // steps, shape[-1])
    lead = (0,) * (len(shape) - 2)
    return pl.BlockSpec(block, lambda i: lead + (i, 0))


def _proj_call(x2, g, wt, gb, head_g, to_cast, tm, batch, seq):
    m, d = x2.shape
    hd = HEADS * DV
    nqk = HEADS * DQK
    assert tm % CHUNK == 0 and seq % tm == 0
    tiles_per_seq = seq // tm
    steps = m // tm
    row = lambda n: pl.BlockSpec((tm, n), lambda i: (i, 0))
    chunk_major = lambda n: pl.BlockSpec((1, tm // CHUNK, n, CHUNK),
                                         lambda i: (i // tiles_per_seq, i % tiles_per_seq, 0, 0))
    cast_specs = [_row_slab_spec(a.shape, steps) for a in to_cast]
    triu = jnp.asarray(np.triu(np.ones((CHUNK, CHUNK), np.float32)), BF16)
    ones = jnp.asarray(np.ones((CHUNK, CHUNK), np.float32), BF16)
    outs = pl.pallas_call(
        functools.partial(_proj_kernel, n_cast=len(to_cast)),
        grid=(steps,),
        in_specs=[row(d), _const_spec((1, d)), _const_spec(wt.shape),
                  _const_spec((N_GATES, 1)), _const_spec((1, hd)),
                  _const_spec((CHUNK, CHUNK)), _const_spec((CHUNK, CHUNK))] + cast_specs,
        out_specs=[row(nqk), chunk_major(nqk), row(hd), row(hd), row(GATE_LANES),
                   chunk_major(ROW_TABLE_ROWS)] + cast_specs,
        out_shape=[jax.ShapeDtypeStruct((m, nqk), BF16),
                   jax.ShapeDtypeStruct((batch, seq // CHUNK, nqk, CHUNK), BF16),
                   jax.ShapeDtypeStruct((m, hd), BF16),
                   jax.ShapeDtypeStruct((m, hd), BF16),
                   jax.ShapeDtypeStruct((m, GATE_LANES), F32),
                   jax.ShapeDtypeStruct((batch, seq // CHUNK, ROW_TABLE_ROWS, CHUNK), F32)]
                  + [jax.ShapeDtypeStruct(a.shape, BF16) for a in to_cast],
        compiler_params=_params(1),
        name="mlstm_proj",
    )(x2, g, wt, gb, head_g, triu, ones, *to_cast)
    return outs[:6], outs[6:]


def _mlstm_kernel(q_ref, kt_ref, v_ref, og_ref, tab_ref, row_ref, out_ref,
                  u_ref, sf_ref, sb_ref):
    nc = kt_ref.shape[1]
    ones_blk = jnp.ones((CHUNK, DV), BF16)
    zero_k = jnp.zeros((DQK, CHUNK), BF16)
    zero_state = jnp.zeros((DQK, 2 * DV), BF16)
    zero_q = jnp.zeros((HALF, 2 * DQK), BF16)
    t_idx = lax.broadcasted_iota(jnp.int32, (HALF, HALF), 0)
    s_idx = lax.broadcasted_iota(jnp.int32, (HALF, HALF), 1)
    lower = s_idx <= t_idx
    upper = s_idx >= t_idx

    def head_steps(h):
        pair_lanes = slice((h // 2) * 2 * DQK, (h // 2 + 1) * 2 * DQK)
        k_rows = slice(h * DQK, (h + 1) * DQK)
        v_lanes = slice(h * DV, (h + 1) * DV)

        def pad_rows(a, zero, h=h):
            return jnp.concatenate([a, zero] if h % 2 == 0 else [zero, a], axis=0)

        def chunk_rows(c):
            return pl.ds(pl.multiple_of(c * CHUNK, CHUNK), CHUNK)

        def vaug(rows, v_lanes=v_lanes):
            return jnp.concatenate([v_ref[rows, v_lanes], ones_blk], axis=1)

        def row(c, group, h=h):
            return row_ref[0, c, group * HEADS + h:group * HEADS + h + 1, :]

        def col(tab, group, h=h):
            return tab[:, group * HEADS + h:group * HEADS + h + 1]

        def contrib_step(c, carry):
            kt = kt_ref[0, c, k_rows, :].astype(F32)
            kw = jnp.concatenate([(kt * row(c, R_WKF)).astype(BF16),
                                  (kt * row(c, R_WKB)).astype(BF16)], axis=0)
            u_ref[c] = _dot(kw, vaug(chunk_rows(c)))
            return carry

        def chunk_scalar(c, group, h=h):
            return row_ref[0, c, group * HEADS + h:group * HEADS + h + 1, 0:1]

        def run_scans():
            state = jnp.zeros((DQK, 2 * DV), F32)
            for c in range(nc):
                sf_ref[c] = state.astype(BF16)
                state = chunk_scalar(c, R_DECF) * state + u_ref[c, 0:DQK, :]
            state = jnp.zeros((DQK, 2 * DV), F32)
            for c in reversed(range(nc)):
                sb_ref[c] = state.astype(BF16)
                state = chunk_scalar(c, R_DECB) * state + u_ref[c, DQK:2 * DQK, :]

        def out_step(c, carry):
            rows = chunk_rows(c)
            q = q_ref[rows, pair_lanes]
            s = _dot(q, pad_rows(kt_ref[0, c, k_rows, :], zero_k))
            tab = tab_ref[rows, :]
            c_f = jnp.broadcast_to(col(tab, G_CF), (CHUNK, HALF))
            c_b = jnp.broadcast_to(col(tab, G_CB), (CHUNK, HALF))
            a_f = row(c, R_AF)
            a_b = row(c, R_AB)

            def blk(t_half, s_half, c_col, a_row, mask=None):
                e = c_col[t_half] + a_row[:, s_half]
                if mask is not None:
                    e = jnp.where(mask, e, -jnp.inf)
                return (s[t_half, s_half] * jnp.exp2(e)).astype(BF16)

            qf32 = q.astype(F32)
            q_f = (qf32 * jnp.exp2(c_f[:, :2 * DQK])).astype(BF16)
            q_b = (qf32 * jnp.exp2(c_b[:, :2 * DQK] + chunk_scalar(c, R_TOTB))).astype(BF16)
            va = vaug(rows)
            rhs_lo = jnp.concatenate([va[LO], pad_rows(sf_ref[c], zero_state)], axis=0)
            rhs_hi = jnp.concatenate([va[HI], pad_rows(sb_ref[c], zero_state)], axis=0)
            zq = zero_q
            lhs_lo = jnp.concatenate(
                [jnp.concatenate([blk(LO, LO, c_f, a_f, lower), q_f[LO]], axis=1),
                 jnp.concatenate([blk(HI, LO, c_f, a_f), q_f[HI]], axis=1),
                 jnp.concatenate([blk(LO, LO, c_b, a_b, upper), zq], axis=1)], axis=0)
            lhs_hi = jnp.concatenate(
                [jnp.concatenate([blk(HI, HI, c_f, a_f, lower), zq], axis=1),
                 jnp.concatenate([blk(LO, HI, c_b, a_b), q_b[LO]], axis=1),
                 jnp.concatenate([blk(HI, HI, c_b, a_b, upper), q_b[HI]], axis=1)], axis=0)
            r_lo = _dot(lhs_lo, rhs_lo)
            r_hi = _dot(lhs_hi, rhs_hi)
            nd_f = jnp.concatenate([r_lo[0:HALF], r_lo[HALF:2 * HALF] + r_hi[0:HALF]], axis=0)
            nd_b = jnp.concatenate([r_lo[2 * HALF:] + r_hi[HALF:2 * HALF], r_hi[2 * HALF:]], axis=0)
            hh = (nd_f[:, :DV] / jnp.maximum(jnp.abs(nd_f[:, DV:]), 1.0)
                  + nd_b[:, :DV] / jnp.maximum(jnp.abs(nd_b[:, DV:]), 1.0))
            hn = hh * lax.rsqrt(jnp.mean(hh * hh, axis=-1, keepdims=True) + EPS)
            gated = hn * og_ref[rows, v_lanes].astype(F32)
            out_ref[rows, v_lanes] = gated.astype(BF16)
            return carry

        return contrib_step, run_scans, out_step

    steps = [head_steps(h) for h in range(HEADS)]
    lax.fori_loop(0, nc, steps[0][0], 0, unroll=MLSTM_UNROLL)
    steps[0][1]()
    for h in range(HEADS):
        out_step = steps[h][2]
        if h + 1 < HEADS:
            next_contrib = steps[h + 1][0]
            lax.fori_loop(0, nc, lambda c, carry: next_contrib(c, out_step(c, carry)), 0,
                          unroll=MLSTM_UNROLL)
            steps[h + 1][1]()
        else:
            lax.fori_loop(0, nc, out_step, 0, unroll=MLSTM_UNROLL)


def _mlstm_call(q, kt, v, og, tab, rowtab, batch, seq):
    hd = HEADS * DV
    nqk = HEADS * DQK
    nc = seq // CHUNK
    seq_blk = lambda n: pl.BlockSpec((seq, n), lambda b: (b, 0))
    return pl.pallas_call(
        _mlstm_kernel,
        grid=(batch,),
        in_specs=[seq_blk(nqk), pl.BlockSpec((1, nc, nqk, CHUNK), lambda b: (b, 0, 0, 0)),
                  seq_blk(hd), seq_blk(hd), seq_blk(GATE_LANES),
                  pl.BlockSpec((1, nc, ROW_TABLE_ROWS, CHUNK), lambda b: (b, 0, 0, 0))],
        out_specs=seq_blk(hd),
        out_shape=jax.ShapeDtypeStruct((batch * seq, hd), BF16),
        scratch_shapes=[pltpu.VMEM((nc, 2 * DQK, 2 * DV), F32),
                        pltpu.VMEM((nc, DQK, 2 * DV), BF16), pltpu.VMEM((nc, DQK, 2 * DV), BF16)],
        compiler_params=_params(1),
        name="mlstm_core",
    )(q, kt, v, og, tab, rowtab)


def _mlp_kernel(*refs, has_scale, out_norm, ff_chunk):
    refs = list(refs)
    x_ref, a_ref, wo_ref = refs[:3]
    del refs[:3]
    sc_ref = refs.pop(0) if has_scale else None
    g_ref, w1_ref, w2_ref = refs[:3]
    del refs[:3]
    fg_ref = refs.pop(0) if out_norm else None
    out_ref = refs.pop(0)

    r = _dot(a_ref[...], wo_ref[...])
    if has_scale:
        r = r * sc_ref[...]
    x1 = x_ref[...] + r
    u = _rms(x1, g_ref[...]).astype(BF16)
    y = x1
    for c in range(w1_ref.shape[1] // ff_chunk):
        cols = slice(c * ff_chunk, (c + 1) * ff_chunk)
        hid = jnp.square(jnp.maximum(_dot(u, w1_ref[:, cols]), 0.0)).astype(BF16)
        y = y + _dot(hid, w2_ref[cols, :])
    if out_norm == "final":
        y = _rms(y, fg_ref[...])
    elif out_norm == "next":
        (u_next_ref,) = refs
        u_next_ref[...] = _rms(y, fg_ref[...]).astype(BF16)
    out_ref[...] = y


def _layer_spec(shape, layer):
    zeros = (0,) * (len(shape) - 1)
    return pl.BlockSpec((None,) + tuple(shape[1:]), lambda *_: (layer,) + zeros,
                        pipeline_mode=pl.Buffered(1))


def _mlp_call(x2, a, wo, scale, g, w1, w2, layer, out_g, out_norm, tm, ff_chunk=1024):
    m, d = x2.shape
    row = lambda n: pl.BlockSpec((tm, n), lambda i: (i, 0))
    args = [x2, a, wo]
    specs = [row(d), row(a.shape[1]), _const_spec(wo.shape)]
    if scale is not None:
        args.append(scale)
        specs.append(_const_spec((1, d)))
    args += [g, w1, w2]
    specs += [_const_spec((1, d)), _layer_spec(w1.shape, layer), _layer_spec(w2.shape, layer)]
    assert out_norm in ("final", "next")
    args.append(out_g)
    specs.append(_const_spec((1, d)))
    out_specs, out_shape = [row(d)], [jax.ShapeDtypeStruct((m, d), F32)]
    if out_norm == "next":
        out_specs.append(row(d))
        out_shape.append(jax.ShapeDtypeStruct((m, d), BF16))
    body = functools.partial(_mlp_kernel, has_scale=scale is not None, out_norm=out_norm,
                             ff_chunk=ff_chunk)
    outs = pl.pallas_call(
        body,
        grid=(m // tm,),
        in_specs=specs,
        out_specs=out_specs,
        out_shape=out_shape,
        input_output_aliases={0: 0} if out_norm == "final" else {},
        compiler_params=_params(1),
        name="mlp",
    )(*args)
    return outs[0] if out_norm == "final" else tuple(outs)


def _pool_kernel(u_ref, win_ref, wgrp_ref, out_ref, a_scr, *, proj_rows, pool_rows):
    seq, d = u_ref.shape
    gdim = d // len(POOL_WINDOWS)
    a_scr[0:POOL_HALO, :] = jnp.zeros((POOL_HALO, d), F32)
    a_scr[POOL_HALO + seq:, :] = jnp.zeros((POOL_HALO, d), F32)
    for t in range(seq // proj_rows):
        rows = slice(t * proj_rows, (t + 1) * proj_rows)
        a_scr[POOL_HALO + t * proj_rows:POOL_HALO + (t + 1) * proj_rows, :] = _dot(u_ref[rows, :],
                                                                                   win_ref[...])

    n_tiles = seq // pool_rows
    n = pool_rows + 2 * POOL_HALO
    centre = slice(POOL_HALO, POOL_HALO + pool_rows)
    for t in range(n_tiles):
        t0 = t * pool_rows
        for gi, w in enumerate(POOL_WINDOWS):
            cols = slice(gi * gdim, (gi + 1) * gdim)
            xw = a_scr[t0:t0 + n, cols]
            fwd = xw
            span = 1
            while 2 * span < w:
                fwd = fwd + pltpu.roll(fwd, n - span, axis=0)
                span *= 2
            s = fwd + pltpu.roll(fwd, w // 2, axis=0)
            if 0 < t < n_tiles - 1:
                mean = s[centre, :] * (1.0 / w)
            else:
                pos = t0 + lax.broadcasted_iota(jnp.int32, (pool_rows, gdim), 0)
                count = jnp.minimum(pos + w // 2, seq) - jnp.maximum(pos - w // 2, 0)
                mean = s[centre, :] / count.astype(F32)
            pooled = mean - xw[centre, :]
            mixed = _dot(pooled.astype(BF16), wgrp_ref[gi])
            out_ref[t0:t0 + pool_rows, cols] = mixed.astype(BF16)


def _pool_call(u, w_in, w_group, batch, seq, proj_rows=512, pool_rows=256):
    d = u.shape[1]
    seq_blk = pl.BlockSpec((seq, d), lambda b: (b, 0))
    body = functools.partial(_pool_kernel, proj_rows=proj_rows, pool_rows=pool_rows)
    return pl.pallas_call(
        body,
        grid=(batch,),
        in_specs=[seq_blk, _const_spec(w_in.shape), _const_spec(w_group.shape)],
        out_specs=seq_blk,
        out_shape=jax.ShapeDtypeStruct((batch * seq, d), BF16),
        scratch_shapes=[pltpu.VMEM((seq + 2 * POOL_HALO, d), F32)],
        compiler_params=_params(1),
        name="pool_mixer",
    )(u, w_in, w_group)


def kernel(x, mix_norm_g, mlp_norm_g, mlstm_w_in, mlstm_gate_b, mlstm_head_g, mlstm_w_out,
           pool_w_in, pool_w_group, pool_w_out, pool_scale, mlp_w1, mlp_w2, final_norm_g):
    batch, seq, d = x.shape
    assert seq % CHUNK == 0 and d == HEADS * DV and HALF == 2 * DQK
    m = batch * seq
    tm = 1024
    x2 = x.reshape(m, d)
    vec = lambda a: a.reshape(1, -1).astype(F32)

    w_in_t = mlstm_w_in[0].T
    gb = mlstm_gate_b[0].astype(F32).reshape(N_GATES, 1)
    n_groups, gdim = pool_w_group.shape[1], pool_w_group.shape[2]

    later_f32 = [mlp_w1, mlp_w2, mlstm_w_out[0], pool_w_in[0], pool_w_out[0],
                 pool_w_group[0].reshape(n_groups * gdim, gdim)]
    (q, kt, v, og, tab, rowtab), (w1, w2, w_out, p_in, p_out, p_group) = _proj_call(
        x2, vec(mix_norm_g[0]), w_in_t, gb, vec(mlstm_head_g[0]), later_f32, tm, batch, seq)
    hmix = _mlstm_call(q, kt, v, og, tab, rowtab, batch, seq)
    x2, u_pool = _mlp_call(x2, hmix, w_out, None, vec(mlp_norm_g[0]), w1, w2, 0,
                           vec(mix_norm_g[1]), "next", tm)

    mixed = _pool_call(u_pool, p_in, p_group.reshape(n_groups, gdim, gdim), batch, seq)
    x2 = _mlp_call(x2, mixed, p_out, vec(pool_scale[0]), vec(mlp_norm_g[1]),
                   w1, w2, 1, vec(final_norm_g), "final", tm)
    return x2.reshape(batch, seq, d)
```
